```python
import math
import numpy as np
import jax
import jax.numpy as jnp
from jax import lax

D_MODEL = 2048
BATCH = 8
SEQ = 4096
DEPTH = 1

HEAD_DIM = 128
N_SB_HEADS = 8
N_NSA_HEADS = 8
N_NSA_KV = 2
NSA_GROUP = N_NSA_HEADS // N_NSA_KV
SB_WIDTH = N_SB_HEADS * HEAD_DIM
NSA_WIDTH = N_NSA_HEADS * HEAD_DIM
KV_WIDTH = N_NSA_KV * HEAD_DIM
MIX_WIDTH = SB_WIDTH + NSA_WIDTH
N_GATES = 3 * N_NSA_HEADS
IN_SPLITS = (SB_WIDTH, SB_WIDTH, SB_WIDTH, NSA_WIDTH, KV_WIDTH, KV_WIDTH, KV_WIDTH, KV_WIDTH, KV_WIDTH, KV_WIDTH, N_GATES)
IN_COLS = sum(IN_SPLITS)
Q_BLOCK = 128
CMP_BLOCK = 32
CMP_STRIDE = 16
CMP_HIDDEN = 256
SEL_BLOCK = 64
N_SELECT = 16
SEL_Q_CHUNK = 32
WINDOW = 512
N_BUCKETS = 32
MAX_DISTANCE = 128
PEER_HEADS = 8
PEER_N_KEYS = 128
PEER_N_EXPERTS = PEER_N_KEYS * PEER_N_KEYS
PEER_D_KEY = 256
PEER_TOPK = 16
PEER_TOKEN_CHUNK = 128
EPS = 1e-6
NEG = -1e30
BIG = 1e30

kernel_name = 'hybrid_sb_nsa_peer_layer'


def rms_norm(x, g):
    xf = x.astype(jnp.float32)
    y = xf * lax.rsqrt(jnp.mean(xf * xf, axis=-1, keepdims=True) + EPS)
    return (y * g.astype(jnp.float32)).astype(x.dtype)


def masked_softmax(s, valid):
    p = jax.nn.softmax(jnp.where(valid, s, NEG), axis=-1)
    return jnp.where(valid, p, 0.0)


def t5_bucket(dist):
    max_exact = N_BUCKETS // 2
    d = jnp.maximum(dist, 0)
    log_ratio = jnp.log(jnp.maximum(d, 1).astype(jnp.float32) / max_exact) / math.log(MAX_DISTANCE / max_exact)
    large = jnp.minimum(max_exact + (log_ratio * (N_BUCKETS - max_exact)).astype(jnp.int32), N_BUCKETS - 1)
    return jnp.where(d < max_exact, d, large)


def rel_bias_grid(table, dist):
    b = jnp.take(table, t5_bucket(dist), axis=0)
    return jnp.moveaxis(b, -1, 0).reshape(N_NSA_KV, NSA_GROUP, *dist.shape)


def stick_breaking_attention(q, k, v):
    B, H, T, Dh = q.shape
    scale = Dh ** -0.5
    kpos = jnp.arange(T)

    def block(i):
        qb = lax.dynamic_slice_in_dim(q, i * Q_BLOCK, Q_BLOCK, axis=2)
        qpos = i * Q_BLOCK + jnp.arange(Q_BLOCK)
        z = jnp.einsum('bhqd,bhkd->bhqk', qb, k, preferred_element_type=jnp.float32) * scale
        before = kpos[None, :] < qpos[:, None]
        log_keep = jnp.where(before, -jax.nn.softplus(z), 0.0)
        log_between = lax.cumsum(log_keep, axis=3, reverse=True) - log_keep
        a = jnp.where(before, jnp.exp(jax.nn.log_sigmoid(z) + log_between), 0.0)
        return jnp.einsum('bhqk,bhkd->bqhd', a.astype(v.dtype), v)

    out = lax.map(block, jnp.arange(T // Q_BLOCK))
    return jnp.moveaxis(out, 0, 1).reshape(B, T, H * Dh)


def compress_blocks(kv, pe, w1, w2):
    B, T, G, Dh = kv.shape
    nc = (T - CMP_BLOCK) // CMP_STRIDE + 1
    idx = jnp.arange(nc)[:, None] * CMP_STRIDE + jnp.arange(CMP_BLOCK)[None, :]
    blocks = kv[:, idx] + pe[None, None, :, None, :]
    flat = jnp.moveaxis(blocks, 3, 2).reshape(B, nc, G, CMP_BLOCK * Dh)
    return jax.nn.gelu(flat @ w1) @ w2


def block_overlap(nc, nb):
    c0 = np.arange(nc)[:, None] * CMP_STRIDE
    s0 = np.arange(nb)[None, :] * SEL_BLOCK
    ov = np.minimum(c0 + CMP_BLOCK, s0 + SEL_BLOCK) - np.maximum(c0, s0)
    return jnp.asarray(np.clip(ov, 0, None).astype(np.float32) / np.float32(CMP_BLOCK))


def compressed_attention(q, kc, vc, table):
    B, G, R, T, Dh = q.shape
    nc = kc.shape[2]
    nb = T // SEL_BLOCK
    scale = Dh ** -0.5
    c_end = jnp.arange(nc) * CMP_STRIDE + CMP_BLOCK - 1
    overlap = block_overlap(nc, nb)

    def block(i):
        qb = lax.dynamic_slice_in_dim(q, i * Q_BLOCK, Q_BLOCK, axis=3)
        qpos = i * Q_BLOCK + jnp.arange(Q_BLOCK)
        dist = qpos[:, None] - c_end[None, :]
        s = jnp.einsum('bgrqd,bgcd->bgrqc', qb, kc, preferred_element_type=jnp.float32) * scale + rel_bias_grid(table, dist)
        p = masked_softmax(s, dist >= 0)
        o = jnp.einsum('bgrqc,bgcd->bgrqd', p.astype(vc.dtype), vc)
        imp = jnp.einsum('bgrqc,cn->bgqn', p, overlap)
        return o, imp

    o, imp = lax.map(block, jnp.arange(T // Q_BLOCK))
    return (jnp.moveaxis(o, 0, 3).reshape(B, G, R, T, Dh), jnp.moveaxis(imp, 0, 2).reshape(B, G, T, nb))


def selected_attention(q, k, v, imp, table):
    B, G, R, T, Dh = q.shape
    nb = T // SEL_BLOCK
    n_sel = min(N_SELECT, nb)
    scale = Dh ** -0.5
    blk = jnp.arange(nb)
    cur = jnp.arange(T) // SEL_BLOCK
    forced = (blk[None, :] == 0) | (blk[None, :] == cur[:, None]) | (blk[None, :] == cur[:, None] - 1)
    future = blk[None, :] > cur[:, None]
    score = jnp.where(forced, BIG, jnp.where(future, NEG, imp))
    _, sel = lax.top_k(score, n_sel)
    kb = k.reshape(B, G, nb, SEL_BLOCK, Dh)
    vb = v.reshape(B, G, nb, SEL_BLOCK, Dh)
    gather = jax.vmap(jax.vmap(lambda blocks, ix: blocks[ix]))
    tab_gr = jnp.swapaxes(table.reshape(N_BUCKETS, G, R), 0, 1)
    g_index = jnp.arange(G)[None, :, None, None]
    n_keys = n_sel * SEL_BLOCK

    def chunk(i):
        qc = lax.dynamic_slice_in_dim(q, i * SEL_Q_CHUNK, SEL_Q_CHUNK, axis=3)
        ic = lax.dynamic_slice_in_dim(sel, i * SEL_Q_CHUNK, SEL_Q_CHUNK, axis=2)
        qpos = i * SEL_Q_CHUNK + jnp.arange(SEL_Q_CHUNK)
        kg = gather(kb, ic).reshape(B, G, SEL_Q_CHUNK, n_keys, Dh)
        vg = gather(vb, ic).reshape(B, G, SEL_Q_CHUNK, n_keys, Dh)
        kpos = (ic[..., None] * SEL_BLOCK + jnp.arange(SEL_BLOCK)).reshape(B, G, SEL_Q_CHUNK, n_keys)
        dist = qpos[None, None, :, None] - kpos
        bias = jnp.moveaxis(tab_gr[g_index, t5_bucket(dist)], -1, 2)
        s = jnp.einsum('bgrqd,bgqkd->bgrqk', qc, kg, preferred_element_type=jnp.float32) * scale + bias
        p = masked_softmax(s, (dist >= 0)[:, :, None])
        return jnp.einsum('bgrqk,bgqkd->bgrqd', p.astype(vg.dtype), vg)

    o = lax.map(chunk, jnp.arange(T // SEL_Q_CHUNK))
    return jnp.moveaxis(o, 0, 3).reshape(B, G, R, T, Dh)


def window_attention(q, k, v, table):
    B, G, R, T, Dh = q.shape
    scale = Dh ** -0.5
    span = WINDOW + Q_BLOCK
    kp = jnp.pad(k, ((0, 0), (0, 0), (WINDOW, 0), (0, 0)))
    vp = jnp.pad(v, ((0, 0), (0, 0), (WINDOW, 0), (0, 0)))

    def block(i):
        qb = lax.dynamic_slice_in_dim(q, i * Q_BLOCK, Q_BLOCK, axis=3)
        kb = lax.dynamic_slice_in_dim(kp, i * Q_BLOCK, span, axis=2)
        vb = lax.dynamic_slice_in_dim(vp, i * Q_BLOCK, span, axis=2)
        qpos = i * Q_BLOCK + jnp.arange(Q_BLOCK)
        kpos = i * Q_BLOCK - WINDOW + jnp.arange(span)
        dist = qpos[:, None] - kpos[None, :]
        valid = (dist >= 0) & (dist < WINDOW) & (kpos[None, :] >= 0)
        s = jnp.einsum('bgrqd,bgkd->bgrqk', qb, kb, preferred_element_type=jnp.float32) * scale + rel_bias_grid(table, dist)
        p = masked_softmax(s, valid)
        return jnp.einsum('bgrqk,bgkd->bgrqd', p.astype(vb.dtype), vb)

    o = lax.map(block, jnp.arange(T // Q_BLOCK))
    return jnp.moveaxis(o, 0, 3).reshape(B, G, R, T, Dh)


def hybrid_mixer(h, w_in, cmp_pe, cmp_k_w1, cmp_k_w2, cmp_v_w1, cmp_v_w2, q_norm_g, k_norm_g,
                 rel_bias_table, sb_out_g, nsa_out_g, w_out):
    B, T, _ = h.shape
    G, R, Dh = N_NSA_KV, NSA_GROUP, HEAD_DIM
    cuts = [int(c) for c in np.cumsum(IN_SPLITS)[:-1]]
    q_sb, k_sb, v_sb, q_nsa, k_cmp, v_cmp, k_slc, v_slc, k_win, v_win, gate_logits = jnp.split(h @ w_in, cuts, axis=-1)

    sb_heads = lambda t: t.reshape(B, T, N_SB_HEADS, Dh).transpose(0, 2, 1, 3)
    o_sb = stick_breaking_attention(sb_heads(q_sb), sb_heads(k_sb), sb_heads(v_sb))

    kv_heads = lambda t: t.reshape(B, T, G, Dh)
    to_bgtd = lambda t: t.transpose(0, 2, 1, 3)
    q = rms_norm(q_nsa.reshape(B, T, G, R, Dh), q_norm_g).transpose(0, 2, 3, 1, 4)
    kc = to_bgtd(rms_norm(compress_blocks(kv_heads(k_cmp), cmp_pe, cmp_k_w1, cmp_k_w2), k_norm_g))
    vc = to_bgtd(compress_blocks(kv_heads(v_cmp), cmp_pe, cmp_v_w1, cmp_v_w2))
    o_cmp, importance = compressed_attention(q, kc, vc, rel_bias_table)
    o_slc = selected_attention(q, to_bgtd(rms_norm(kv_heads(k_slc), k_norm_g)), to_bgtd(kv_heads(v_slc)), importance, rel_bias_table)
    o_win = window_attention(q, to_bgtd(rms_norm(kv_heads(k_win), k_norm_g)), to_bgtd(kv_heads(v_win)), rel_bias_table)
    gates = jax.nn.sigmoid(gate_logits).reshape(B, T, G, R, 3).transpose(0, 2, 3, 1, 4)
    o_nsa = gates[..., 0:1] * o_cmp + gates[..., 1:2] * o_slc + gates[..., 2:3] * o_win
    o_nsa = o_nsa.transpose(0, 3, 1, 2, 4).reshape(B, T, NSA_WIDTH)

    mixed = jnp.concatenate([rms_norm(o_sb, sb_out_g), rms_norm(o_nsa, nsa_out_g)], axis=-1)
    return mixed @ w_out


def peer_ffn(h, w_query, sub_keys, expert_u, expert_v):
    B, T, D = h.shape
    n_tok = B * T
    C = PEER_TOKEN_CHUNK
    half = PEER_D_KEY // 2

    def chunk(xc):
        qc = (xc @ w_query).reshape(C, PEER_HEADS, 2, half)
        s = jnp.einsum('chpd,hpnd->chpn', qc, sub_keys, preferred_element_type=jnp.float32)
        top_s, top_i = lax.top_k(s, PEER_TOPK)
        cand_s = (top_s[:, :, 0, :, None] + top_s[:, :, 1, None, :]).reshape(C, PEER_HEADS, PEER_TOPK * PEER_TOPK)
        cand_i = (top_i[:, :, 0, :, None] * PEER_N_KEYS + top_i[:, :, 1, None, :]).reshape(C, PEER_HEADS, PEER_TOPK * PEER_TOPK)
        best_s, best_pos = lax.top_k(cand_s, PEER_TOPK)
        expert_idx = jnp.take_along_axis(cand_i, best_pos, axis=-1)
        g = jax.nn.softmax(best_s, axis=-1)
        u = expert_u[expert_idx]
        vv = expert_v[expert_idx]
        act = jax.nn.gelu(jnp.einsum('chkd,cd->chk', u, xc, preferred_element_type=jnp.float32))
        return jnp.einsum('chk,chkd->cd', (g * act).astype(vv.dtype), vv)

    out = lax.map(chunk, h.reshape(n_tok // C, C, D))
    return out.reshape(B, T, D)


def setup_inputs(seed: int = 0) -> dict:
    key = jax.random.key(seed)
    ks = jax.random.split(key, 19)
    f32 = jnp.float32
    L = DEPTH

    def nrm(k, shape, scale):
        return jax.random.normal(k, shape, f32) * scale

    def gain(k, shape):
        return 1.0 + 0.02 * jax.random.normal(k, shape, f32)

    return {
        'x': nrm(ks[0], (BATCH, SEQ, D_MODEL), 1.0),
        'norm1_g': gain(ks[1], (L, D_MODEL)),
        'w_in': nrm(ks[2], (L, D_MODEL, IN_COLS), D_MODEL ** -0.5),
        'cmp_pe': nrm(ks[3], (L, CMP_BLOCK, HEAD_DIM), 0.2),
        'cmp_k_w1': nrm(ks[4], (L, CMP_BLOCK * HEAD_DIM, CMP_HIDDEN), (CMP_BLOCK * HEAD_DIM) ** -0.5),
        'cmp_k_w2': nrm(ks[5], (L, CMP_HIDDEN, HEAD_DIM), CMP_HIDDEN ** -0.5),
        'cmp_v_w1': nrm(ks[6], (L, CMP_BLOCK * HEAD_DIM, CMP_HIDDEN), (CMP_BLOCK * HEAD_DIM) ** -0.5),
        'cmp_v_w2': nrm(ks[7], (L, CMP_HIDDEN, HEAD_DIM), CMP_HIDDEN ** -0.5),
        'q_norm_g': gain(ks[8], (L, HEAD_DIM)),
        'k_norm_g': gain(ks[9], (L, HEAD_DIM)),
        'rel_bias_table': nrm(ks[10], (N_BUCKETS, N_NSA_HEADS), 0.5),
        'sb_out_g': gain(ks[11], (L, SB_WIDTH)),
        'nsa_out_g': gain(ks[12], (L, NSA_WIDTH)),
        'w_out': nrm(ks[13], (L, MIX_WIDTH, D_MODEL), MIX_WIDTH ** -0.5),
        'norm2_g': gain(ks[14], (L, D_MODEL)),
        'peer_w_query': nrm(ks[15], (L, D_MODEL, PEER_HEADS * PEER_D_KEY), D_MODEL ** -0.5),
        'peer_sub_keys': nrm(ks[16], (L, PEER_HEADS, 2, PEER_N_KEYS, PEER_D_KEY // 2), (PEER_D_KEY // 2) ** -0.5),
        'peer_expert_u': nrm(ks[17], (L, PEER_N_EXPERTS, D_MODEL), D_MODEL ** -0.5),
        'peer_expert_v': nrm(ks[18], (L, PEER_N_EXPERTS, D_MODEL), 0.5),
    }


def reference(x, norm1_g, w_in, cmp_pe, cmp_k_w1, cmp_k_w2, cmp_v_w1, cmp_v_w2, q_norm_g, k_norm_g,
              rel_bias_table, sb_out_g, nsa_out_g, w_out, norm2_g, peer_w_query, peer_sub_keys,
              peer_expert_u, peer_expert_v):
    for l in range(DEPTH):
        h = rms_norm(x, norm1_g[l])
        x = x + hybrid_mixer(h, w_in[l], cmp_pe[l], cmp_k_w1[l], cmp_k_w2[l], cmp_v_w1[l], cmp_v_w2[l],
                             q_norm_g[l], k_norm_g[l], rel_bias_table, sb_out_g[l], nsa_out_g[l], w_out[l])
        h = rms_norm(x, norm2_g[l])
        x = x + peer_ffn(h, peer_w_query[l], peer_sub_keys[l], peer_expert_u[l], peer_expert_v[l])
    return x
```

```python
import functools
import math

import numpy as np
import jax
import jax.numpy as jnp
from jax import lax
from jax.experimental import pallas as pl
from jax.experimental.pallas import tpu as pltpu

D_MODEL = 2048
HEAD_DIM = 128
N_SB_HEADS = 8
N_NSA_HEADS = 8
N_NSA_KV = 2
NSA_GROUP = N_NSA_HEADS // N_NSA_KV
SB_WIDTH = N_SB_HEADS * HEAD_DIM
NSA_WIDTH = N_NSA_HEADS * HEAD_DIM
KV_WIDTH = N_NSA_KV * HEAD_DIM
MIX_WIDTH = SB_WIDTH + NSA_WIDTH
N_GATES = 3 * N_NSA_HEADS
IN_SPLITS = (SB_WIDTH, SB_WIDTH, SB_WIDTH, NSA_WIDTH, KV_WIDTH, KV_WIDTH, KV_WIDTH, KV_WIDTH, KV_WIDTH, KV_WIDTH, N_GATES)
IN_COLS = sum(IN_SPLITS)
Q_BLOCK = 128
CMP_BLOCK = 32
CMP_STRIDE = 16
CMP_HIDDEN = 256
SEL_BLOCK = 64
N_SELECT = 16
SEL_Q_CHUNK = 32
WINDOW = 512
N_BUCKETS = 32
MAX_DISTANCE = 128
PEER_HEADS = 8
PEER_N_KEYS = 128
PEER_D_KEY = 256
PEER_TOPK = 16
PEER_TOKEN_CHUNK = 128
EPS = 1e-6
NEG = -1e30
BIG = 1e30

VMEM_LIMIT_BYTES = 56 * 1024 * 1024


def _norm_matmul_kernel(x_ref, g_ref, w_ref, o_ref, h_ref):
    @pl.when(pl.program_id(1) == 0)
    def _():
        x = x_ref[...]
        ms = jnp.mean(x * x, axis=-1, keepdims=True)
        h_ref[...] = (x * lax.rsqrt(ms + EPS) * g_ref[...]).astype(jnp.bfloat16)

    o_ref[...] = jnp.dot(h_ref[...], w_ref[...], preferred_element_type=jnp.float32)


def norm_matmul(x, g, w, tm=512, tn=640):
    m, k = x.shape
    n = w.shape[1]
    assert m % tm == 0 and n % tn == 0
    return pl.pallas_call(
        _norm_matmul_kernel,
        grid=(m // tm, n // tn),
        in_specs=[
            pl.BlockSpec((tm, k), lambda i, j: (i, 0)),
            pl.BlockSpec((1, k), lambda i, j: (0, 0)),
            pl.BlockSpec((k, tn), lambda i, j: (0, j)),
        ],
        out_specs=pl.BlockSpec((tm, tn), lambda i, j: (i, j)),
        out_shape=jax.ShapeDtypeStruct((m, n), jnp.float32),
        scratch_shapes=[pltpu.VMEM((tm, k), jnp.bfloat16)],
        compiler_params=pltpu.CompilerParams(
            dimension_semantics=("parallel", "arbitrary"),
            vmem_limit_bytes=VMEM_LIMIT_BYTES),
        name="norm_matmul",
    )(x, g.reshape(1, k), w)


def _matmul_residual_kernel(a_ref, w_ref, r_ref, o_ref):
    o_ref[...] = r_ref[...] + jnp.dot(a_ref[...].astype(jnp.bfloat16), w_ref[...],
                                      preferred_element_type=jnp.float32)


def matmul_residual(a, w, r, tm=512, tn=512):
    m, k = a.shape
    n = w.shape[1]
    assert m % tm == 0 and n % tn == 0
    return pl.pallas_call(
        _matmul_residual_kernel,
        grid=(m // tm, n // tn),
        in_specs=[
            pl.BlockSpec((tm, k), lambda i, j: (i, 0)),
            pl.BlockSpec((k, tn), lambda i, j: (0, j)),
            pl.BlockSpec((tm, tn), lambda i, j: (i, j)),
        ],
        out_specs=pl.BlockSpec((tm, tn), lambda i, j: (i, j)),
        out_shape=jax.ShapeDtypeStruct((m, n), jnp.float32),
        compiler_params=pltpu.CompilerParams(
            dimension_semantics=("parallel", "arbitrary"),
            vmem_limit_bytes=VMEM_LIMIT_BYTES),
        name="matmul_residual",
    )(a, w, r)


def rms_norm(x, g):
    xf = x.astype(jnp.float32)
    y = xf * lax.rsqrt(jnp.mean(xf * xf, axis=-1, keepdims=True) + EPS)
    return (y * g.astype(jnp.float32)).astype(x.dtype)


def masked_softmax(s, valid):
    p = jax.nn.softmax(jnp.where(valid, s, NEG), axis=-1)
    return jnp.where(valid, p, 0.0)


def t5_bucket(dist):
    max_exact = N_BUCKETS // 2
    d = jnp.maximum(dist, 0)
    log_ratio = jnp.log(jnp.maximum(d, 1).astype(jnp.float32) / max_exact) / math.log(MAX_DISTANCE / max_exact)
    large = jnp.minimum(max_exact + (log_ratio * (N_BUCKETS - max_exact)).astype(jnp.int32), N_BUCKETS - 1)
    return jnp.where(d < max_exact, d, large)


def rel_bias_grid(table, dist):
    b = jnp.take(table, t5_bucket(dist), axis=0)
    return jnp.moveaxis(b, -1, 0).reshape(N_NSA_KV, NSA_GROUP, *dist.shape)


def stick_breaking_attention(q, k, v):
    B, H, T, Dh = q.shape
    scale = Dh ** -0.5
    kpos = jnp.arange(T)

    def block(i):
        qb = lax.dynamic_slice_in_dim(q, i * Q_BLOCK, Q_BLOCK, axis=2)
        qpos = i * Q_BLOCK + jnp.arange(Q_BLOCK)
        z = jnp.einsum('bhqd,bhkd->bhqk', qb, k, preferred_element_type=jnp.float32) * scale
        before = kpos[None, :] < qpos[:, None]
        log_keep = jnp.where(before, -jax.nn.softplus(z), 0.0)
        log_between = lax.cumsum(log_keep, axis=3, reverse=True) - log_keep
        a = jnp.where(before, jnp.exp(jax.nn.log_sigmoid(z) + log_between), 0.0)
        return jnp.einsum('bhqk,bhkd->bqhd', a.astype(v.dtype), v)

    out = lax.map(block, jnp.arange(T // Q_BLOCK))
    return jnp.moveaxis(out, 0, 1).reshape(B, T, H * Dh)


def compress_blocks(kv, pe, w1, w2):
    B, T, G, Dh = kv.shape
    nc = (T - CMP_BLOCK) // CMP_STRIDE + 1
    idx = jnp.arange(nc)[:, None] * CMP_STRIDE + jnp.arange(CMP_BLOCK)[None, :]
    blocks = kv[:, idx] + pe[None, None, :, None, :]
    flat = jnp.moveaxis(blocks, 3, 2).reshape(B, nc, G, CMP_BLOCK * Dh)
    return jax.nn.gelu(flat @ w1) @ w2


def block_overlap(nc, nb):
    c0 = np.arange(nc)[:, None] * CMP_STRIDE
    s0 = np.arange(nb)[None, :] * SEL_BLOCK
    ov = np.minimum(c0 + CMP_BLOCK, s0 + SEL_BLOCK) - np.maximum(c0, s0)
    return jnp.asarray(np.clip(ov, 0, None).astype(np.float32) / np.float32(CMP_BLOCK))


def compressed_attention(q, kc, vc, table):
    B, G, R, T, Dh = q.shape
    nc = kc.shape[2]
    nb = T // SEL_BLOCK
    scale = Dh ** -0.5
    c_end = jnp.arange(nc) * CMP_STRIDE + CMP_BLOCK - 1
    overlap = block_overlap(nc, nb)

    def block(i):
        qb = lax.dynamic_slice_in_dim(q, i * Q_BLOCK, Q_BLOCK, axis=3)
        qpos = i * Q_BLOCK + jnp.arange(Q_BLOCK)
        dist = qpos[:, None] - c_end[None, :]
        s = jnp.einsum('bgrqd,bgcd->bgrqc', qb, kc, preferred_element_type=jnp.float32) * scale + rel_bias_grid(table, dist)
        p = masked_softmax(s, dist >= 0)
        o = jnp.einsum('bgrqc,bgcd->bgrqd', p.astype(vc.dtype), vc)
        imp = jnp.einsum('bgrqc,cn->bgqn', p, overlap)
        return o, imp

    o, imp = lax.map(block, jnp.arange(T // Q_BLOCK))
    return (jnp.moveaxis(o, 0, 3).reshape(B, G, R, T, Dh), jnp.moveaxis(imp, 0, 2).reshape(B, G, T, nb))


def selected_attention(q, k, v, imp, table):
    B, G, R, T, Dh = q.shape
    nb = T // SEL_BLOCK
    n_sel = min(N_SELECT, nb)
    scale = Dh ** -0.5
    blk = jnp.arange(nb)
    cur = jnp.arange(T) // SEL_BLOCK
    forced = (blk[None, :] == 0) | (blk[None, :] == cur[:, None]) | (blk[None, :] == cur[:, None] - 1)
    future = blk[None, :] > cur[:, None]
    score = jnp.where(forced, BIG, jnp.where(future, NEG, imp))
    _, sel = lax.top_k(score, n_sel)
    kb = k.reshape(B, G, nb, SEL_BLOCK, Dh)
    vb = v.reshape(B, G, nb, SEL_BLOCK, Dh)
    gather = jax.vmap(jax.vmap(lambda blocks, ix: blocks[ix]))
    tab_gr = jnp.swapaxes(table.reshape(N_BUCKETS, G, R), 0, 1)
    g_index = jnp.arange(G)[None, :, None, None]
    n_keys = n_sel * SEL_BLOCK

    def chunk(i):
        qc = lax.dynamic_slice_in_dim(q, i * SEL_Q_CHUNK, SEL_Q_CHUNK, axis=3)
        ic = lax.dynamic_slice_in_dim(sel, i * SEL_Q_CHUNK, SEL_Q_CHUNK, axis=2)
        qpos = i * SEL_Q_CHUNK + jnp.arange(SEL_Q_CHUNK)
        kg = gather(kb, ic).reshape(B, G, SEL_Q_CHUNK, n_keys, Dh)
        vg = gather(vb, ic).reshape(B, G, SEL_Q_CHUNK, n_keys, Dh)
        kpos = (ic[..., None] * SEL_BLOCK + jnp.arange(SEL_BLOCK)).reshape(B, G, SEL_Q_CHUNK, n_keys)
        dist = qpos[None, None, :, None] - kpos
        bias = jnp.moveaxis(tab_gr[g_index, t5_bucket(dist)], -1, 2)
        s = jnp.einsum('bgrqd,bgqkd->bgrqk', qc, kg, preferred_element_type=jnp.float32) * scale + bias
        p = masked_softmax(s, (dist >= 0)[:, :, None])
        return jnp.einsum('bgrqk,bgqkd->bgrqd', p.astype(vg.dtype), vg)

    o = lax.map(chunk, jnp.arange(T // SEL_Q_CHUNK))
    return jnp.moveaxis(o, 0, 3).reshape(B, G, R, T, Dh)


def window_attention(q, k, v, table):
    B, G, R, T, Dh = q.shape
    scale = Dh ** -0.5
    span = WINDOW + Q_BLOCK
    kp = jnp.pad(k, ((0, 0), (0, 0), (WINDOW, 0), (0, 0)))
    vp = jnp.pad(v, ((0, 0), (0, 0), (WINDOW, 0), (0, 0)))

    def block(i):
        qb = lax.dynamic_slice_in_dim(q, i * Q_BLOCK, Q_BLOCK, axis=3)
        kb = lax.dynamic_slice_in_dim(kp, i * Q_BLOCK, span, axis=2)
        vb = lax.dynamic_slice_in_dim(vp, i * Q_BLOCK, span, axis=2)
        qpos = i * Q_BLOCK + jnp.arange(Q_BLOCK)
        kpos = i * Q_BLOCK - WINDOW + jnp.arange(span)
        dist = qpos[:, None] - kpos[None, :]
        valid = (dist >= 0) & (dist < WINDOW) & (kpos[None, :] >= 0)
        s = jnp.einsum('bgrqd,bgkd->bgrqk', qb, kb, preferred_element_type=jnp.float32) * scale + rel_bias_grid(table, dist)
        p = masked_softmax(s, valid)
        return jnp.einsum('bgrqk,bgkd->bgrqd', p.astype(vb.dtype), vb)

    o = lax.map(block, jnp.arange(T // Q_BLOCK))
    return jnp.moveaxis(o, 0, 3).reshape(B, G, R, T, Dh)


def peer_ffn(h, w_query, sub_keys, expert_u, expert_v):
    B, T, D = h.shape
    n_tok = B * T
    C = PEER_TOKEN_CHUNK
    half = PEER_D_KEY // 2

    def chunk(xc):
        qc = (xc @ w_query).reshape(C, PEER_HEADS, 2, half)
        s = jnp.einsum('chpd,hpnd->chpn', qc, sub_keys, preferred_element_type=jnp.float32)
        top_s, top_i = lax.top_k(s, PEER_TOPK)
        cand_s = (top_s[:, :, 0, :, None] + top_s[:, :, 1, None, :]).reshape(C, PEER_HEADS, PEER_TOPK * PEER_TOPK)
        cand_i = (top_i[:, :, 0, :, None] * PEER_N_KEYS + top_i[:, :, 1, None, :]).reshape(C, PEER_HEADS, PEER_TOPK * PEER_TOPK)
        best_s, best_pos = lax.top_k(cand_s, PEER_TOPK)
        expert_idx = jnp.take_along_axis(cand_i, best_pos, axis=-1)
        g = jax.nn.softmax(best_s, axis=-1)
        u = expert_u[expert_idx]
        vv = expert_v[expert_idx]
        act = jax.nn.gelu(jnp.einsum('chkd,cd->chk', u, xc, preferred_element_type=jnp.float32))
        return jnp.einsum('chk,chkd->cd', (g * act).astype(vv.dtype), vv)

    out = lax.map(chunk, h.reshape(n_tok // C, C, D))
    return out.reshape(B, T, D)


def kernel(x, norm1_g, w_in, cmp_pe, cmp_k_w1, cmp_k_w2, cmp_v_w1, cmp_v_w2, q_norm_g, k_norm_g, rel_bias_table, sb_out_g, nsa_out_g, w_out, norm2_g, peer_w_query, peer_sub_keys, peer_expert_u, peer_expert_v):
    B, T, D = x.shape
    G, R, Dh = N_NSA_KV, NSA_GROUP, HEAD_DIM
    l = 0
    x2 = x.reshape(B * T, D)

    n_pad = (-IN_COLS) % 640
    w_in_p = jnp.pad(w_in[l], ((0, 0), (0, n_pad))).astype(jnp.bfloat16)
    proj = norm_matmul(x2, norm1_g[l], w_in_p)[:, :IN_COLS].reshape(B, T, IN_COLS)

    cuts = [int(c) for c in np.cumsum(IN_SPLITS)[:-1]]
    q_sb, k_sb, v_sb, q_nsa, k_cmp, v_cmp, k_slc, v_slc, k_win, v_win, gate_logits = jnp.split(proj, cuts, axis=-1)

    sb_heads = lambda t: t.reshape(B, T, N_SB_HEADS, Dh).transpose(0, 2, 1, 3)
    o_sb = stick_breaking_attention(sb_heads(q_sb), sb_heads(k_sb), sb_heads(v_sb))

    kv_heads = lambda t: t.reshape(B, T, G, Dh)
    to_bgtd = lambda t: t.transpose(0, 2, 1, 3)
    q = rms_norm(q_nsa.reshape(B, T, G, R, Dh), q_norm_g[l]).transpose(0, 2, 3, 1, 4)
    kc = to_bgtd(rms_norm(compress_blocks(kv_heads(k_cmp), cmp_pe[l], cmp_k_w1[l], cmp_k_w2[l]), k_norm_g[l]))
    vc = to_bgtd(compress_blocks(kv_heads(v_cmp), cmp_pe[l], cmp_v_w1[l], cmp_v_w2[l]))
    o_cmp, importance = compressed_attention(q, kc, vc, rel_bias_table)
    o_slc = selected_attention(q, to_bgtd(rms_norm(kv_heads(k_slc), k_norm_g[l])), to_bgtd(kv_heads(v_slc)), importance, rel_bias_table)
    o_win = window_attention(q, to_bgtd(rms_norm(kv_heads(k_win), k_norm_g[l])), to_bgtd(kv_heads(v_win)), rel_bias_table)
    gates = jax.nn.sigmoid(gate_logits).reshape(B, T, G, R, 3).transpose(0, 2, 3, 1, 4)
    o_nsa = gates[..., 0:1] * o_cmp + gates[..., 1:2] * o_slc + gates[..., 2:3] * o_win
    o_nsa = o_nsa.transpose(0, 3, 1, 2, 4).reshape(B, T, NSA_WIDTH)

    mixed = jnp.concatenate([rms_norm(o_sb, sb_out_g[l]), rms_norm(o_nsa, nsa_out_g[l])], axis=-1)
    x1 = matmul_residual(mixed.reshape(B * T, MIX_WIDTH), w_out[l].astype(jnp.bfloat16), x2)

    h2 = rms_norm(x1, norm2_g[l]).reshape(B, T, D)
    out = x1.reshape(B, T, D) + peer_ffn(h2, peer_w_query[l], peer_sub_keys[l], peer_expert_u[l], peer_expert_v[l])
    return out
```

```python
import functools
import math

import numpy as np
import jax
import jax.numpy as jnp
from jax import lax
from jax.experimental import pallas as pl
from jax.experimental.pallas import tpu as pltpu

D_MODEL = 2048
HEAD_DIM = 128
N_SB_HEADS = 8
N_NSA_HEADS = 8
N_NSA_KV = 2
NSA_GROUP = N_NSA_HEADS // N_NSA_KV
SB_WIDTH = N_SB_HEADS * HEAD_DIM
NSA_WIDTH = N_NSA_HEADS * HEAD_DIM
KV_WIDTH = N_NSA_KV * HEAD_DIM
MIX_WIDTH = SB_WIDTH + NSA_WIDTH
N_GATES = 3 * N_NSA_HEADS
IN_SPLITS = (SB_WIDTH, SB_WIDTH, SB_WIDTH, NSA_WIDTH, KV_WIDTH, KV_WIDTH, KV_WIDTH, KV_WIDTH, KV_WIDTH, KV_WIDTH, N_GATES)
IN_COLS = sum(IN_SPLITS)
Q_BLOCK = 128
CMP_BLOCK = 32
CMP_STRIDE = 16
CMP_HIDDEN = 256
SEL_BLOCK = 64
N_SELECT = 16
SEL_Q_CHUNK = 32
WINDOW = 512
N_BUCKETS = 32
MAX_DISTANCE = 128
PEER_HEADS = 8
PEER_N_KEYS = 128
PEER_D_KEY = 256
PEER_TOPK = 16
PEER_TOKEN_CHUNK = 128
EPS = 1e-6
NEG = -1e30
BIG = 1e30

VMEM_LIMIT_BYTES = 56 * 1024 * 1024


def _norm_matmul_kernel(x_ref, g_ref, w_ref, o_ref, h_ref):
    @pl.when(pl.program_id(1) == 0)
    def _():
        x = x_ref[...]
        ms = jnp.mean(x * x, axis=-1, keepdims=True)
        h_ref[...] = (x * lax.rsqrt(ms + EPS) * g_ref[...]).astype(jnp.bfloat16)

    o_ref[...] = jnp.dot(h_ref[...], w_ref[...], preferred_element_type=jnp.float32)


def norm_matmul(x, g, w, tm=512, tn=640):
    m, k = x.shape
    n = w.shape[1]
    assert m % tm == 0 and n % tn == 0
    return pl.pallas_call(
        _norm_matmul_kernel,
        grid=(m // tm, n // tn),
        in_specs=[
            pl.BlockSpec((tm, k), lambda i, j: (i, 0)),
            pl.BlockSpec((1, k), lambda i, j: (0, 0)),
            pl.BlockSpec((k, tn), lambda i, j: (0, j)),
        ],
        out_specs=pl.BlockSpec((tm, tn), lambda i, j: (i, j)),
        out_shape=jax.ShapeDtypeStruct((m, n), jnp.float32),
        scratch_shapes=[pltpu.VMEM((tm, k), jnp.bfloat16)],
        compiler_params=pltpu.CompilerParams(
            dimension_semantics=("parallel", "arbitrary"),
            vmem_limit_bytes=VMEM_LIMIT_BYTES),
        name="norm_matmul",
    )(x, g.reshape(1, k), w)


def _matmul_residual_kernel(a_ref, w_ref, r_ref, o_ref):
    o_ref[...] = r_ref[...] + jnp.dot(a_ref[...].astype(jnp.bfloat16), w_ref[...],
                                      preferred_element_type=jnp.float32)


def matmul_residual(a, w, r, tm=512, tn=512):
    m, k = a.shape
    n = w.shape[1]
    assert m % tm == 0 and n % tn == 0
    return pl.pallas_call(
        _matmul_residual_kernel,
        grid=(m // tm, n // tn),
        in_specs=[
            pl.BlockSpec((tm, k), lambda i, j: (i, 0)),
            pl.BlockSpec((k, tn), lambda i, j: (0, j)),
            pl.BlockSpec((tm, tn), lambda i, j: (i, j)),
        ],
        out_specs=pl.BlockSpec((tm, tn), lambda i, j: (i, j)),
        out_shape=jax.ShapeDtypeStruct((m, n), jnp.float32),
        compiler_params=pltpu.CompilerParams(
            dimension_semantics=("parallel", "arbitrary"),
            vmem_limit_bytes=VMEM_LIMIT_BYTES),
        name="matmul_residual",
    )(a, w, r)


def _bias_saturation_distance():
    max_exact = N_BUCKETS // 2
    return int(math.ceil(max_exact * (MAX_DISTANCE / max_exact) ** ((N_BUCKETS - 1 - max_exact) / (N_BUCKETS - max_exact))))


def _cmp_attn_kernel(q_ref, kc_ref, vc_ref, bias_ref, ov_ref, o_ref, selt_ref, *, n_cmp, n_sel):
    r, qb, dh = q_ref.shape
    ncp = kc_ref.shape[0]
    nb = ov_ref.shape[0]
    qi = pl.program_id(2)
    q = q_ref[...].reshape(r * qb, dh)
    s = lax.dot_general(q, kc_ref[...], (((1,), (1,)), ((), ())), preferred_element_type=jnp.float32)
    s = s.reshape(r, qb, ncp) + bias_ref[...]
    tpos = qi * qb + lax.broadcasted_iota(jnp.int32, (qb, ncp), 0)
    c = lax.broadcasted_iota(jnp.int32, (qb, ncp), 1)
    valid = ((tpos - (c * CMP_STRIDE + CMP_BLOCK - 1)) >= 0) & (c < n_cmp)
    s = jnp.where(valid[None], s, NEG)
    m = jnp.max(s, axis=-1, keepdims=True)
    e = jnp.where(valid[None], jnp.exp(s - m), 0.0)
    l = jnp.sum(e, axis=-1, keepdims=True)
    p = e * jnp.where(l > 0.0, 1.0 / l, 0.0)
    o = jnp.dot(p.reshape(r * qb, ncp).astype(jnp.bfloat16), vc_ref[...], preferred_element_type=jnp.float32)
    o_ref[...] = o.reshape(r, qb, dh)

    psum = jnp.sum(p, axis=0)
    p_hi = psum.astype(jnp.bfloat16)
    p_lo = (psum - p_hi.astype(jnp.float32)).astype(jnp.bfloat16)
    nt = (((1,), (1,)), ((), ()))
    imp_t = (lax.dot_general(ov_ref[...], p_hi, nt, preferred_element_type=jnp.float32)
             + lax.dot_general(ov_ref[...], p_lo, nt, preferred_element_type=jnp.float32))

    blk = lax.broadcasted_iota(jnp.int32, (nb, qb), 0)
    cur = (qi * qb + lax.broadcasted_iota(jnp.int32, (nb, qb), 1)) // SEL_BLOCK
    forced = (blk == 0) | (blk == cur) | (blk == cur - 1)
    score = jnp.where(forced, BIG, jnp.where(blk > cur, NEG, imp_t))
    rank = jnp.zeros((nb, qb), jnp.float32)
    for mblk in range(nb):
        row = score[mblk:mblk + 1, :]
        ahead = (row > score) | ((row == score) & (blk > mblk))
        rank = rank + jnp.where(ahead, 1.0, 0.0)
    selt_ref[...] = jnp.where(rank < float(n_sel), 1.0, 0.0)


def compressed_attention_select(q, kc, vc, cmp_bias, n_cmp):
    B, G, R, T, Dh = q.shape
    ncp = kc.shape[2]
    nb = T // SEL_BLOCK
    n_sel = min(N_SELECT, nb)
    c0 = np.arange(ncp)[None, :] * CMP_STRIDE
    s0 = np.arange(nb)[:, None] * SEL_BLOCK
    ov = np.clip(np.minimum(c0 + CMP_BLOCK, s0 + SEL_BLOCK) - np.maximum(c0, s0), 0, None).astype(np.float32) / np.float32(CMP_BLOCK)
    ov = np.where(np.arange(ncp)[None, :] < n_cmp, ov, 0.0)
    ov_t = jnp.asarray(ov, dtype=jnp.bfloat16)
    qb = Q_BLOCK
    return pl.pallas_call(
        functools.partial(_cmp_attn_kernel, n_cmp=n_cmp, n_sel=n_sel),
        grid=(B, G, T // qb),
        in_specs=[
            pl.BlockSpec((None, None, R, qb, Dh), lambda b, g, i: (b, g, 0, i, 0)),
            pl.BlockSpec((None, None, ncp, Dh), lambda b, g, i: (b, g, 0, 0)),
            pl.BlockSpec((None, None, ncp, Dh), lambda b, g, i: (b, g, 0, 0)),
            pl.BlockSpec((None, R, qb, ncp), lambda b, g, i: (g, 0, i, 0)),
            pl.BlockSpec((nb, ncp), lambda b, g, i: (0, 0)),
        ],
        out_specs=[
            pl.BlockSpec((None, None, R, qb, Dh), lambda b, g, i: (b, g, 0, i, 0)),
            pl.BlockSpec((None, None, nb, qb), lambda b, g, i: (b, g, 0, i)),
        ],
        out_shape=[
            jax.ShapeDtypeStruct((B, G, R, T, Dh), jnp.float32),
            jax.ShapeDtypeStruct((B, G, nb, T), jnp.float32),
        ],
        compiler_params=pltpu.CompilerParams(
            dimension_semantics=("parallel", "parallel", "arbitrary"),
            vmem_limit_bytes=VMEM_LIMIT_BYTES),
        name="cmp_attention_select",
    )(q, kc, vc, cmp_bias, ov_t)


def _band_attn_kernel(*refs, mode):
    if mode == "slc":
        q_ref, k_ref, v_ref, bias_ref, sel_ref, o_ref, m_ref, l_ref, acc_ref = refs
    else:
        q_ref, k_ref, v_ref, bias_ref, o_ref, m_ref, l_ref, acc_ref = refs
    r, qb, dh = q_ref.shape
    qi = pl.program_id(2)
    q = q_ref[...].reshape(r * qb, dh)
    m_ref[...] = jnp.full(m_ref.shape, -jnp.inf, jnp.float32)
    l_ref[...] = jnp.zeros(l_ref.shape, jnp.float32)
    acc_ref[...] = jnp.zeros(acc_ref.shape, jnp.float32)
    row = lax.broadcasted_iota(jnp.int32, (qb, qb), 0)
    col = lax.broadcasted_iota(jnp.int32, (qb, qb), 1)
    if mode == "slc":
        nb = sel_ref.shape[1]
        sel = sel_ref[...]
        blk_of_col = lax.broadcasted_iota(jnp.int32, (nb, qb), 1) // SEL_BLOCK
        blk_row = lax.broadcasted_iota(jnp.int32, (nb, qb), 0)
        n_steps = qi + 1
    else:
        n_steps = jnp.minimum(qi, WINDOW // qb) + 1

    def body(step, carry):
        off = step
        kj = qi - off
        start = pl.multiple_of(kj * qb, qb)
        k = k_ref[pl.ds(start, qb), :]
        v = v_ref[pl.ds(start, qb), :]
        s = lax.dot_general(q, k, (((1,), (1,)), ((), ())), preferred_element_type=jnp.float32)
        dist = off * qb + row - col
        if mode == "slc":
            expand = jnp.where(blk_row == blk_of_col + kj * (qb // SEL_BLOCK), 1.0, 0.0).astype(jnp.bfloat16)
            picked = jnp.dot(sel, expand, preferred_element_type=jnp.float32)
            valid = (picked > 0.5) & (dist >= 0)
        else:
            valid = (dist >= 0) & (dist < WINDOW)
        addmask = jnp.where(valid, 0.0, NEG)
        s = s.reshape(r, qb, qb) + (bias_ref[jnp.minimum(off, 2)] + addmask[None])
        s = s.reshape(r * qb, qb)
        m_prev = m_ref[...]
        m_new = jnp.maximum(m_prev, jnp.max(s, axis=-1, keepdims=True))
        alpha = jnp.exp(m_prev - m_new)
        p = jnp.exp(s - m_new)
        l_ref[...] = alpha * l_ref[...] + jnp.sum(p, axis=-1, keepdims=True)
        acc_ref[...] = alpha * acc_ref[...] + jnp.dot(p.astype(jnp.bfloat16), v, preferred_element_type=jnp.float32)
        m_ref[...] = m_new
        return carry

    lax.fori_loop(0, n_steps, body, 0)
    o_ref[...] = (acc_ref[...] / l_ref[...]).reshape(r, qb, dh)


def band_attention(q, k, v, tile_bias, sel=None):
    B, G, R, T, Dh = q.shape
    qb = Q_BLOCK
    mode = "win" if sel is None else "slc"
    in_specs = [
        pl.BlockSpec((None, None, R, qb, Dh), lambda b, g, i: (b, g, 0, i, 0)),
        pl.BlockSpec((None, None, T, Dh), lambda b, g, i: (b, g, 0, 0)),
        pl.BlockSpec((None, None, T, Dh), lambda b, g, i: (b, g, 0, 0)),
        pl.BlockSpec((3, None, R, qb, qb), lambda b, g, i: (0, g, 0, 0, 0)),
    ]
    args = [q, k, v, tile_bias]
    if sel is not None:
        nb = sel.shape[-1]
        in_specs.append(pl.BlockSpec((None, None, qb, nb), lambda b, g, i: (b, g, i, 0)))
        args.append(sel)
    return pl.pallas_call(
        functools.partial(_band_attn_kernel, mode=mode),
        grid=(B, G, T // qb),
        in_specs=in_specs,
        out_specs=pl.BlockSpec((None, None, R, qb, Dh), lambda b, g, i: (b, g, 0, i, 0)),
        out_shape=jax.ShapeDtypeStruct((B, G, R, T, Dh), jnp.float32),
        scratch_shapes=[
            pltpu.VMEM((R * qb, 1), jnp.float32),
            pltpu.VMEM((R * qb, 1), jnp.float32),
            pltpu.VMEM((R * qb, Dh), jnp.float32),
        ],
        compiler_params=pltpu.CompilerParams(
            dimension_semantics=("parallel", "parallel", "arbitrary"),
            vmem_limit_bytes=VMEM_LIMIT_BYTES),
        name="band_attention_" + mode,
    )(*args)


def nsa_bias_tables(table, T, ncp):
    assert _bias_saturation_distance() <= Q_BLOCK + 1
    G, R = N_NSA_KV, NSA_GROUP
    c_end = jnp.arange(ncp) * CMP_STRIDE + CMP_BLOCK - 1
    dist_c = jnp.arange(T)[:, None] - c_end[None, :]
    cmp_bias = jnp.moveaxis(jnp.take(table, t5_bucket(dist_c), axis=0), -1, 0).reshape(G, R, T, ncp)
    tl = jnp.arange(Q_BLOCK)
    dist_t = jnp.arange(3)[:, None, None] * Q_BLOCK + tl[None, :, None] - tl[None, None, :]
    tile_bias = jnp.moveaxis(jnp.take(table, t5_bucket(dist_t), axis=0), -1, 1).reshape(3, G, R, Q_BLOCK, Q_BLOCK)
    return cmp_bias, tile_bias


def rms_norm(x, g):
    xf = x.astype(jnp.float32)
    y = xf * lax.rsqrt(jnp.mean(xf * xf, axis=-1, keepdims=True) + EPS)
    return (y * g.astype(jnp.float32)).astype(x.dtype)


def masked_softmax(s, valid):
    p = jax.nn.softmax(jnp.where(valid, s, NEG), axis=-1)
    return jnp.where(valid, p, 0.0)


def t5_bucket(dist):
    max_exact = N_BUCKETS // 2
    d = jnp.maximum(dist, 0)
    log_ratio = jnp.log(jnp.maximum(d, 1).astype(jnp.float32) / max_exact) / math.log(MAX_DISTANCE / max_exact)
    large = jnp.minimum(max_exact + (log_ratio * (N_BUCKETS - max_exact)).astype(jnp.int32), N_BUCKETS - 1)
    return jnp.where(d < max_exact, d, large)


def rel_bias_grid(table, dist):
    b = jnp.take(table, t5_bucket(dist), axis=0)
    return jnp.moveaxis(b, -1, 0).reshape(N_NSA_KV, NSA_GROUP, *dist.shape)


def stick_breaking_attention(q, k, v):
    B, H, T, Dh = q.shape
    scale = Dh ** -0.5
    kpos = jnp.arange(T)

    def block(i):
        qb = lax.dynamic_slice_in_dim(q, i * Q_BLOCK, Q_BLOCK, axis=2)
        qpos = i * Q_BLOCK + jnp.arange(Q_BLOCK)
        z = jnp.einsum('bhqd,bhkd->bhqk', qb, k, preferred_element_type=jnp.float32) * scale
        before = kpos[None, :] < qpos[:, None]
        log_keep = jnp.where(before, -jax.nn.softplus(z), 0.0)
        log_between = lax.cumsum(log_keep, axis=3, reverse=True) - log_keep
        a = jnp.where(before, jnp.exp(jax.nn.log_sigmoid(z) + log_between), 0.0)
        return jnp.einsum('bhqk,bhkd->bqhd', a.astype(v.dtype), v)

    out = lax.map(block, jnp.arange(T // Q_BLOCK))
    return jnp.moveaxis(out, 0, 1).reshape(B, T, H * Dh)


def compress_blocks(kv, pe, w1, w2):
    B, T, G, Dh = kv.shape
    nc = (T - CMP_BLOCK) // CMP_STRIDE + 1
    idx = jnp.arange(nc)[:, None] * CMP_STRIDE + jnp.arange(CMP_BLOCK)[None, :]
    blocks = kv[:, idx] + pe[None, None, :, None, :]
    flat = jnp.moveaxis(blocks, 3, 2).reshape(B, nc, G, CMP_BLOCK * Dh)
    return jax.nn.gelu(flat @ w1) @ w2


def block_overlap(nc, nb):
    c0 = np.arange(nc)[:, None] * CMP_STRIDE
    s0 = np.arange(nb)[None, :] * SEL_BLOCK
    ov = np.minimum(c0 + CMP_BLOCK, s0 + SEL_BLOCK) - np.maximum(c0, s0)
    return jnp.asarray(np.clip(ov, 0, None).astype(np.float32) / np.float32(CMP_BLOCK))


def compressed_attention(q, kc, vc, table):
    B, G, R, T, Dh = q.shape
    nc = kc.shape[2]
    nb = T // SEL_BLOCK
    scale = Dh ** -0.5
    c_end = jnp.arange(nc) * CMP_STRIDE + CMP_BLOCK - 1
    overlap = block_overlap(nc, nb)

    def block(i):
        qb = lax.dynamic_slice_in_dim(q, i * Q_BLOCK, Q_BLOCK, axis=3)
        qpos = i * Q_BLOCK + jnp.arange(Q_BLOCK)
        dist = qpos[:, None] - c_end[None, :]
        s = jnp.einsum('bgrqd,bgcd->bgrqc', qb, kc, preferred_element_type=jnp.float32) * scale + rel_bias_grid(table, dist)
        p = masked_softmax(s, dist >= 0)
        o = jnp.einsum('bgrqc,bgcd->bgrqd', p.astype(vc.dtype), vc)
        imp = jnp.einsum('bgrqc,cn->bgqn', p, overlap)
        return o, imp

    o, imp = lax.map(block, jnp.arange(T // Q_BLOCK))
    return (jnp.moveaxis(o, 0, 3).reshape(B, G, R, T, Dh), jnp.moveaxis(imp, 0, 2).reshape(B, G, T, nb))


def selected_attention(q, k, v, imp, table):
    B, G, R, T, Dh = q.shape
    nb = T // SEL_BLOCK
    n_sel = min(N_SELECT, nb)
    scale = Dh ** -0.5
    blk = jnp.arange(nb)
    cur = jnp.arange(T) // SEL_BLOCK
    forced = (blk[None, :] == 0) | (blk[None, :] == cur[:, None]) | (blk[None, :] == cur[:, None] - 1)
    future = blk[None, :] > cur[:, None]
    score = jnp.where(forced, BIG, jnp.where(future, NEG, imp))
    _, sel = lax.top_k(score, n_sel)
    kb = k.reshape(B, G, nb, SEL_BLOCK, Dh)
    vb = v.reshape(B, G, nb, SEL_BLOCK, Dh)
    gather = jax.vmap(jax.vmap(lambda blocks, ix: blocks[ix]))
    tab_gr = jnp.swapaxes(table.reshape(N_BUCKETS, G, R), 0, 1)
    g_index = jnp.arange(G)[None, :, None, None]
    n_keys = n_sel * SEL_BLOCK

    def chunk(i):
        qc = lax.dynamic_slice_in_dim(q, i * SEL_Q_CHUNK, SEL_Q_CHUNK, axis=3)
        ic = lax.dynamic_slice_in_dim(sel, i * SEL_Q_CHUNK, SEL_Q_CHUNK, axis=2)
        qpos = i * SEL_Q_CHUNK + jnp.arange(SEL_Q_CHUNK)
        kg = gather(kb, ic).reshape(B, G, SEL_Q_CHUNK, n_keys, Dh)
        vg = gather(vb, ic).reshape(B, G, SEL_Q_CHUNK, n_keys, Dh)
        kpos = (ic[..., None] * SEL_BLOCK + jnp.arange(SEL_BLOCK)).reshape(B, G, SEL_Q_CHUNK, n_keys)
        dist = qpos[None, None, :, None] - kpos
        bias = jnp.moveaxis(tab_gr[g_index, t5_bucket(dist)], -1, 2)
        s = jnp.einsum('bgrqd,bgqkd->bgrqk', qc, kg, preferred_element_type=jnp.float32) * scale + bias
        p = masked_softmax(s, (dist >= 0)[:, :, None])
        return jnp.einsum('bgrqk,bgqkd->bgrqd', p.astype(vg.dtype), vg)

    o = lax.map(chunk, jnp.arange(T // SEL_Q_CHUNK))
    return jnp.moveaxis(o, 0, 3).reshape(B, G, R, T, Dh)


def window_attention(q, k, v, table):
    B, G, R, T, Dh = q.shape
    scale = Dh ** -0.5
    span = WINDOW + Q_BLOCK
    kp = jnp.pad(k, ((0, 0), (0, 0), (WINDOW, 0), (0, 0)))
    vp = jnp.pad(v, ((0, 0), (0, 0), (WINDOW, 0), (0, 0)))

    def block(i):
        qb = lax.dynamic_slice_in_dim(q, i * Q_BLOCK, Q_BLOCK, axis=3)
        kb = lax.dynamic_slice_in_dim(kp, i * Q_BLOCK, span, axis=2)
        vb = lax.dynamic_slice_in_dim(vp, i * Q_BLOCK, span, axis=2)
        qpos = i * Q_BLOCK + jnp.arange(Q_BLOCK)
        kpos = i * Q_BLOCK - WINDOW + jnp.arange(span)
        dist = qpos[:, None] - kpos[None, :]
        valid = (dist >= 0) & (dist < WINDOW) & (kpos[None, :] >= 0)
        s = jnp.einsum('bgrqd,bgkd->bgrqk', qb, kb, preferred_element_type=jnp.float32) * scale + rel_bias_grid(table, dist)
        p = masked_softmax(s, valid)
        return jnp.einsum('bgrqk,bgkd->bgrqd', p.astype(vb.dtype), vb)

    o = lax.map(block, jnp.arange(T // Q_BLOCK))
    return jnp.moveaxis(o, 0, 3).reshape(B, G, R, T, Dh)


def peer_ffn(h, w_query, sub_keys, expert_u, expert_v):
    B, T, D = h.shape
    n_tok = B * T
    C = PEER_TOKEN_CHUNK
    half = PEER_D_KEY // 2

    def chunk(xc):
        qc = (xc @ w_query).reshape(C, PEER_HEADS, 2, half)
        s = jnp.einsum('chpd,hpnd->chpn', qc, sub_keys, preferred_element_type=jnp.float32)
        top_s, top_i = lax.top_k(s, PEER_TOPK)
        cand_s = (top_s[:, :, 0, :, None] + top_s[:, :, 1, None, :]).reshape(C, PEER_HEADS, PEER_TOPK * PEER_TOPK)
        cand_i = (top_i[:, :, 0, :, None] * PEER_N_KEYS + top_i[:, :, 1, None, :]).reshape(C, PEER_HEADS, PEER_TOPK * PEER_TOPK)
        best_s, best_pos = lax.top_k(cand_s, PEER_TOPK)
        expert_idx = jnp.take_along_axis(cand_i, best_pos, axis=-1)
        g = jax.nn.softmax(best_s, axis=-1)
        u = expert_u[expert_idx]
        vv = expert_v[expert_idx]
        act = jax.nn.gelu(jnp.einsum('chkd,cd->chk', u, xc, preferred_element_type=jnp.float32))
        return jnp.einsum('chk,chkd->cd', (g * act).astype(vv.dtype), vv)

    out = lax.map(chunk, h.reshape(n_tok // C, C, D))
    return out.reshape(B, T, D)


def kernel(x, norm1_g, w_in, cmp_pe, cmp_k_w1, cmp_k_w2, cmp_v_w1, cmp_v_w2, q_norm_g, k_norm_g, rel_bias_table, sb_out_g, nsa_out_g, w_out, norm2_g, peer_w_query, peer_sub_keys, peer_expert_u, peer_expert_v):
    B, T, D = x.shape
    G, R, Dh = N_NSA_KV, NSA_GROUP, HEAD_DIM
    l = 0
    x2 = x.reshape(B * T, D)

    n_pad = (-IN_COLS) % 640
    w_in_p = jnp.pad(w_in[l], ((0, 0), (0, n_pad))).astype(jnp.bfloat16)
    proj = norm_matmul(x2, norm1_g[l], w_in_p)[:, :IN_COLS].reshape(B, T, IN_COLS)

    cuts = [int(c) for c in np.cumsum(IN_SPLITS)[:-1]]
    q_sb, k_sb, v_sb, q_nsa, k_cmp, v_cmp, k_slc, v_slc, k_win, v_win, gate_logits = jnp.split(proj, cuts, axis=-1)

    sb_heads = lambda t: t.reshape(B, T, N_SB_HEADS, Dh).transpose(0, 2, 1, 3)
    o_sb = stick_breaking_attention(sb_heads(q_sb), sb_heads(k_sb), sb_heads(v_sb))

    kv_heads = lambda t: t.reshape(B, T, G, Dh)
    to_bgtd = lambda t: t.transpose(0, 2, 1, 3)
    q = rms_norm(q_nsa.reshape(B, T, G, R, Dh), q_norm_g[l]).transpose(0, 2, 3, 1, 4)
    kc = to_bgtd(rms_norm(compress_blocks(kv_heads(k_cmp), cmp_pe[l], cmp_k_w1[l], cmp_k_w2[l]), k_norm_g[l]))
    vc = to_bgtd(compress_blocks(kv_heads(v_cmp), cmp_pe[l], cmp_v_w1[l], cmp_v_w2[l]))
    bf16 = jnp.bfloat16
    qs = (q * (Dh ** -0.5)).astype(bf16)
    n_cmp = kc.shape[2]
    ncp = -(-n_cmp // 128) * 128
    pad_c = lambda t: jnp.pad(t, ((0, 0), (0, 0), (0, ncp - n_cmp), (0, 0))).astype(bf16)
    cmp_bias, tile_bias = nsa_bias_tables(rel_bias_table, T, ncp)
    o_cmp, sel_t = compressed_attention_select(qs, pad_c(kc), pad_c(vc), cmp_bias, n_cmp)
    sel = jnp.swapaxes(sel_t, 2, 3).astype(bf16)
    o_slc = band_attention(qs, to_bgtd(rms_norm(kv_heads(k_slc), k_norm_g[l])).astype(bf16),
                           to_bgtd(kv_heads(v_slc)).astype(bf16), tile_bias, sel)
    o_win = band_attention(qs, to_bgtd(rms_norm(kv_heads(k_win), k_norm_g[l])).astype(bf16),
                           to_bgtd(kv_heads(v_win)).astype(bf16), tile_bias)
    gates = jax.nn.sigmoid(gate_logits).reshape(B, T, G, R, 3).transpose(0, 2, 3, 1, 4)
    o_nsa = gates[..., 0:1] * o_cmp + gates[..., 1:2] * o_slc + gates[..., 2:3] * o_win
    o_nsa = o_nsa.transpose(0, 3, 1, 2, 4).reshape(B, T, NSA_WIDTH)

    mixed = jnp.concatenate([rms_norm(o_sb, sb_out_g[l]), rms_norm(o_nsa, nsa_out_g[l])], axis=-1)
    x1 = matmul_residual(mixed.reshape(B * T, MIX_WIDTH), w_out[l].astype(jnp.bfloat16), x2)

    h2 = rms_norm(x1, norm2_g[l]).reshape(B, T, D)
    out = x1.reshape(B, T, D) + peer_ffn(h2, peer_w_query[l], peer_sub_keys[l], peer_expert_u[l], peer_expert_v[l])
    return out
```

```python
import functools
import math

import numpy as np
import jax
import jax.numpy as jnp
from jax import lax
from jax.experimental import pallas as pl
from jax.experimental.pallas import tpu as pltpu

D_MODEL = 2048
HEAD_DIM = 128
N_SB_HEADS = 8
N_NSA_HEADS = 8
N_NSA_KV = 2
NSA_GROUP = N_NSA_HEADS // N_NSA_KV
SB_WIDTH = N_SB_HEADS * HEAD_DIM
NSA_WIDTH = N_NSA_HEADS * HEAD_DIM
KV_WIDTH = N_NSA_KV * HEAD_DIM
MIX_WIDTH = SB_WIDTH + NSA_WIDTH
N_GATES = 3 * N_NSA_HEADS
IN_SPLITS = (SB_WIDTH, SB_WIDTH, SB_WIDTH, NSA_WIDTH, KV_WIDTH, KV_WIDTH, KV_WIDTH, KV_WIDTH, KV_WIDTH, KV_WIDTH, N_GATES)
IN_COLS = sum(IN_SPLITS)
Q_BLOCK = 128
CMP_BLOCK = 32
CMP_STRIDE = 16
CMP_HIDDEN = 256
SEL_BLOCK = 64
N_SELECT = 16
SEL_Q_CHUNK = 32
WINDOW = 512
N_BUCKETS = 32
MAX_DISTANCE = 128
PEER_HEADS = 8
PEER_N_KEYS = 128
PEER_D_KEY = 256
PEER_TOPK = 16
PEER_TOKEN_CHUNK = 128
EPS = 1e-6
NEG = -1e30
BIG = 1e30

VMEM_LIMIT_BYTES = 56 * 1024 * 1024


def _norm_matmul_kernel(x_ref, g_ref, w_ref, o_ref, h_ref):
    @pl.when(pl.program_id(1) == 0)
    def _():
        x = x_ref[...]
        ms = jnp.mean(x * x, axis=-1, keepdims=True)
        h_ref[...] = (x * lax.rsqrt(ms + EPS) * g_ref[...]).astype(jnp.bfloat16)

    o_ref[...] = jnp.dot(h_ref[...], w_ref[...], preferred_element_type=jnp.float32).astype(o_ref.dtype)


def norm_matmul(x, g, w, tm=512, tn=640, out_dtype=jnp.bfloat16):
    m, k = x.shape
    n = w.shape[1]
    assert m % tm == 0 and n % tn == 0
    return pl.pallas_call(
        _norm_matmul_kernel,
        grid=(m // tm, n // tn),
        in_specs=[
            pl.BlockSpec((tm, k), lambda i, j: (i, 0)),
            pl.BlockSpec((1, k), lambda i, j: (0, 0)),
            pl.BlockSpec((k, tn), lambda i, j: (0, j)),
        ],
        out_specs=pl.BlockSpec((tm, tn), lambda i, j: (i, j)),
        out_shape=jax.ShapeDtypeStruct((m, n), out_dtype),
        scratch_shapes=[pltpu.VMEM((tm, k), jnp.bfloat16)],
        compiler_params=pltpu.CompilerParams(
            dimension_semantics=("parallel", "arbitrary"),
            vmem_limit_bytes=VMEM_LIMIT_BYTES),
        name="norm_matmul",
    )(x, g.reshape(1, k), w)


def _matmul_residual_kernel(a_ref, w_ref, r_ref, o_ref):
    o_ref[...] = r_ref[...] + jnp.dot(a_ref[...].astype(jnp.bfloat16), w_ref[...],
                                      preferred_element_type=jnp.float32)


def matmul_residual(a, w, r, tm=512, tn=512):
    m, k = a.shape
    n = w.shape[1]
    assert m % tm == 0 and n % tn == 0
    return pl.pallas_call(
        _matmul_residual_kernel,
        grid=(m // tm, n // tn),
        in_specs=[
            pl.BlockSpec((tm, k), lambda i, j: (i, 0)),
            pl.BlockSpec((k, tn), lambda i, j: (0, j)),
            pl.BlockSpec((tm, tn), lambda i, j: (i, j)),
        ],
        out_specs=pl.BlockSpec((tm, tn), lambda i, j: (i, j)),
        out_shape=jax.ShapeDtypeStruct((m, n), jnp.float32),
        compiler_params=pltpu.CompilerParams(
            dimension_semantics=("parallel", "arbitrary"),
            vmem_limit_bytes=VMEM_LIMIT_BYTES),
        name="matmul_residual",
    )(a, w, r)


SB_Q_TILE = 256
SB_K_TILE = 128
F32_EXP_UNDERFLOW = -104.0


def _sb_attn_kernel(q_ref, k_ref, v_ref, o_ref, c_ref, acc_ref):
    tq, dh = q_ref.shape
    tk = SB_K_TILE
    qi = pl.program_id(2)
    q = q_ref[...]
    scale = dh ** -0.5
    c_ref[...] = jnp.zeros(c_ref.shape, jnp.float32)
    acc_ref[...] = jnp.zeros(acc_ref.shape, jnp.float32)
    jj = lax.broadcasted_iota(jnp.int32, (2 * tk, 2 * tk), 0) % tk
    ss = lax.broadcasted_iota(jnp.int32, (2 * tk, 2 * tk), 1)
    suffix_and_total = jnp.where((jj > ss) | (ss >= tk), 1.0, 0.0).astype(jnp.bfloat16)
    row = lax.broadcasted_iota(jnp.int32, (tq, tk), 0)
    col = lax.broadcasted_iota(jnp.int32, (tq, tk), 1)

    def tile(kj, masked):
        start = pl.multiple_of(kj * tk, tk)
        k = k_ref[pl.ds(start, tk), :]
        v = v_ref[pl.ds(start, tk), :]
        z = lax.dot_general(q, k, (((1,), (1,)), ((), ())), preferred_element_type=jnp.float32) * scale
        t = jnp.log(1.0 + jnp.exp(-jnp.abs(z)))
        log_keep = -(jnp.maximum(z, 0.0) + t)
        log_beta = jnp.minimum(z, 0.0) - t
        if masked:
            before = (kj * tk + col) < (qi * tq + row)
            log_keep = jnp.where(before, log_keep, 0.0)
        hi = log_keep.astype(jnp.bfloat16)
        lo = (log_keep - hi.astype(jnp.float32)).astype(jnp.bfloat16)
        sums = jnp.dot(jnp.concatenate([hi, lo], axis=1), suffix_and_total, preferred_element_type=jnp.float32)
        c = c_ref[...]
        a = jnp.exp(log_beta + sums[:, :tk] + c)
        if masked:
            a = jnp.where(before, a, 0.0)
        acc_ref[...] += jnp.dot(a.astype(jnp.bfloat16), v, preferred_element_type=jnp.float32)
        c_new = c + sums[:, tk:]
        c_ref[...] = c_new
        return jnp.max(c_new)

    n_diag = tq // tk
    c_max = jnp.float32(0.0)
    for d in range(n_diag - 1, -1, -1):
        c_max = tile(qi * n_diag + d, True)

    def cond(carry):
        kj, c_max = carry
        return (kj >= 0) & (c_max > F32_EXP_UNDERFLOW)

    def body(carry):
        kj, _ = carry
        return kj - 1, tile(kj, False)

    lax.while_loop(cond, body, (qi * n_diag - 1, c_max))
    o_ref[...] = acc_ref[...]


def stick_breaking_attention_pallas(proj, n_heads, q_col, k_col, v_col):
    B, T, _ = proj.shape
    dh = HEAD_DIM
    tq = min(SB_Q_TILE, T)
    return pl.pallas_call(
        _sb_attn_kernel,
        grid=(B, n_heads, T // tq),
        in_specs=[
            pl.BlockSpec((None, tq, dh), lambda b, h, i: (b, i, q_col + h)),
            pl.BlockSpec((None, T, dh), lambda b, h, i: (b, 0, k_col + h)),
            pl.BlockSpec((None, T, dh), lambda b, h, i: (b, 0, v_col + h)),
        ],
        out_specs=pl.BlockSpec((None, tq, dh), lambda b, h, i: (b, i, h)),
        out_shape=jax.ShapeDtypeStruct((B, T, n_heads * dh), jnp.float32),
        scratch_shapes=[
            pltpu.VMEM((tq, SB_K_TILE), jnp.float32),
            pltpu.VMEM((tq, dh), jnp.float32),
        ],
        compiler_params=pltpu.CompilerParams(
            dimension_semantics=("parallel", "parallel", "arbitrary"),
            vmem_limit_bytes=VMEM_LIMIT_BYTES),
        name="sb_attention",
    )(proj, proj, proj)


def _bias_saturation_distance():
    max_exact = N_BUCKETS // 2
    return int(math.ceil(max_exact * (MAX_DISTANCE / max_exact) ** ((N_BUCKETS - 1 - max_exact) / (N_BUCKETS - max_exact))))


def _cmp_attn_kernel(q_ref, kc_ref, vc_ref, bias_ref, ov_ref, o_ref, selt_ref, *, n_cmp, n_sel):
    r, qb, dh = q_ref.shape
    ncp = kc_ref.shape[0]
    nb = ov_ref.shape[0]
    qi = pl.program_id(2)
    q = q_ref[...].reshape(r * qb, dh)
    s = lax.dot_general(q, kc_ref[...], (((1,), (1,)), ((), ())), preferred_element_type=jnp.float32)
    s = s.reshape(r, qb, ncp) + bias_ref[...]
    tpos = qi * qb + lax.broadcasted_iota(jnp.int32, (qb, ncp), 0)
    c = lax.broadcasted_iota(jnp.int32, (qb, ncp), 1)
    valid = ((tpos - (c * CMP_STRIDE + CMP_BLOCK - 1)) >= 0) & (c < n_cmp)
    s = jnp.where(valid[None], s, NEG)
    m = jnp.max(s, axis=-1, keepdims=True)
    e = jnp.where(valid[None], jnp.exp(s - m), 0.0)
    l = jnp.sum(e, axis=-1, keepdims=True)
    p = e * jnp.where(l > 0.0, 1.0 / l, 0.0)
    o = jnp.dot(p.reshape(r * qb, ncp).astype(jnp.bfloat16), vc_ref[...], preferred_element_type=jnp.float32)
    o_ref[...] = o.reshape(r, qb, dh)

    psum = jnp.sum(p, axis=0)
    p_hi = psum.astype(jnp.bfloat16)
    p_lo = (psum - p_hi.astype(jnp.float32)).astype(jnp.bfloat16)
    nt = (((1,), (1,)), ((), ()))
    imp_t = (lax.dot_general(ov_ref[...], p_hi, nt, preferred_element_type=jnp.float32)
             + lax.dot_general(ov_ref[...], p_lo, nt, preferred_element_type=jnp.float32))

    blk = lax.broadcasted_iota(jnp.int32, (nb, qb), 0)
    cur = (qi * qb + lax.broadcasted_iota(jnp.int32, (nb, qb), 1)) // SEL_BLOCK
    forced = (blk == 0) | (blk == cur) | (blk == cur - 1)
    score = jnp.where(forced, BIG, jnp.where(blk > cur, NEG, imp_t))
    rank = jnp.zeros((nb, qb), jnp.float32)
    for mblk in range(nb):
        row = score[mblk:mblk + 1, :]
        ahead = (row > score) | ((row == score) & (blk > mblk))
        rank = rank + jnp.where(ahead, 1.0, 0.0)
    selt_ref[...] = jnp.where(rank < float(n_sel), 1.0, 0.0)


def compressed_attention_select(q, kc, vc, cmp_bias, n_cmp):
    B, G, R, T, Dh = q.shape
    ncp = kc.shape[2]
    nb = T // SEL_BLOCK
    n_sel = min(N_SELECT, nb)
    c0 = np.arange(ncp)[None, :] * CMP_STRIDE
    s0 = np.arange(nb)[:, None] * SEL_BLOCK
    ov = np.clip(np.minimum(c0 + CMP_BLOCK, s0 + SEL_BLOCK) - np.maximum(c0, s0), 0, None).astype(np.float32) / np.float32(CMP_BLOCK)
    ov = np.where(np.arange(ncp)[None, :] < n_cmp, ov, 0.0)
    ov_t = jnp.asarray(ov, dtype=jnp.bfloat16)
    qb = Q_BLOCK
    return pl.pallas_call(
        functools.partial(_cmp_attn_kernel, n_cmp=n_cmp, n_sel=n_sel),
        grid=(B, G, T // qb),
        in_specs=[
            pl.BlockSpec((None, None, R, qb, Dh), lambda b, g, i: (b, g, 0, i, 0)),
            pl.BlockSpec((None, None, ncp, Dh), lambda b, g, i: (b, g, 0, 0)),
            pl.BlockSpec((None, None, ncp, Dh), lambda b, g, i: (b, g, 0, 0)),
            pl.BlockSpec((None, R, qb, ncp), lambda b, g, i: (g, 0, i, 0)),
            pl.BlockSpec((nb, ncp), lambda b, g, i: (0, 0)),
        ],
        out_specs=[
            pl.BlockSpec((None, None, R, qb, Dh), lambda b, g, i: (b, g, 0, i, 0)),
            pl.BlockSpec((None, None, nb, qb), lambda b, g, i: (b, g, 0, i)),
        ],
        out_shape=[
            jax.ShapeDtypeStruct((B, G, R, T, Dh), jnp.float32),
            jax.ShapeDtypeStruct((B, G, nb, T), jnp.float32),
        ],
        compiler_params=pltpu.CompilerParams(
            dimension_semantics=("parallel", "parallel", "arbitrary"),
            vmem_limit_bytes=VMEM_LIMIT_BYTES),
        name="cmp_attention_select",
    )(q, kc, vc, cmp_bias, ov_t)


def _band_attn_kernel(*refs, mode):
    if mode == "slc":
        q_ref, k_ref, v_ref, bias_ref, sel_ref, o_ref, m_ref, l_ref, acc_ref = refs
    else:
        q_ref, k_ref, v_ref, bias_ref, o_ref, m_ref, l_ref, acc_ref = refs
    r, qb, dh = q_ref.shape
    qi = pl.program_id(2)
    q = q_ref[...].reshape(r * qb, dh)
    m_ref[...] = jnp.full(m_ref.shape, -jnp.inf, jnp.float32)
    l_ref[...] = jnp.zeros(l_ref.shape, jnp.float32)
    acc_ref[...] = jnp.zeros(acc_ref.shape, jnp.float32)
    row = lax.broadcasted_iota(jnp.int32, (qb, qb), 0)
    col = lax.broadcasted_iota(jnp.int32, (qb, qb), 1)
    if mode == "slc":
        nb = sel_ref.shape[1]
        sel = sel_ref[...]
        blk_of_col = lax.broadcasted_iota(jnp.int32, (nb, qb), 1) // SEL_BLOCK
        blk_row = lax.broadcasted_iota(jnp.int32, (nb, qb), 0)
        n_steps = qi + 1
    else:
        n_steps = jnp.minimum(qi, WINDOW // qb) + 1

    def body(step, carry):
        off = step
        kj = qi - off
        start = pl.multiple_of(kj * qb, qb)
        k = k_ref[pl.ds(start, qb), :]
        v = v_ref[pl.ds(start, qb), :]
        s = lax.dot_general(q, k, (((1,), (1,)), ((), ())), preferred_element_type=jnp.float32)
        dist = off * qb + row - col
        if mode == "slc":
            expand = jnp.where(blk_row == blk_of_col + kj * (qb // SEL_BLOCK), 1.0, 0.0).astype(jnp.bfloat16)
            picked = jnp.dot(sel, expand, preferred_element_type=jnp.float32)
            valid = (picked > 0.5) & (dist >= 0)
        else:
            valid = (dist >= 0) & (dist < WINDOW)
        addmask = jnp.where(valid, 0.0, NEG)
        s = s.reshape(r, qb, qb) + (bias_ref[jnp.minimum(off, 2)] + addmask[None])
        s = s.reshape(r * qb, qb)
        m_prev = m_ref[...]
        m_new = jnp.maximum(m_prev, jnp.max(s, axis=-1, keepdims=True))
        alpha = jnp.exp(m_prev - m_new)
        p = jnp.exp(s - m_new)
        l_ref[...] = alpha * l_ref[...] + jnp.sum(p, axis=-1, keepdims=True)
        acc_ref[...] = alpha * acc_ref[...] + jnp.dot(p.astype(jnp.bfloat16), v, preferred_element_type=jnp.float32)
        m_ref[...] = m_new
        return carry

    lax.fori_loop(0, n_steps, body, 0)
    o_ref[...] = (acc_ref[...] / l_ref[...]).reshape(r, qb, dh)


def band_attention(q, k, v, tile_bias, sel=None):
    B, G, R, T, Dh = q.shape
    qb = Q_BLOCK
    mode = "win" if sel is None else "slc"
    in_specs = [
        pl.BlockSpec((None, None, R, qb, Dh), lambda b, g, i: (b, g, 0, i, 0)),
        pl.BlockSpec((None, None, T, Dh), lambda b, g, i: (b, g, 0, 0)),
        pl.BlockSpec((None, None, T, Dh), lambda b, g, i: (b, g, 0, 0)),
        pl.BlockSpec((3, None, R, qb, qb), lambda b, g, i: (0, g, 0, 0, 0)),
    ]
    args = [q, k, v, tile_bias]
    if sel is not None:
        nb = sel.shape[-1]
        in_specs.append(pl.BlockSpec((None, None, qb, nb), lambda b, g, i: (b, g, i, 0)))
        args.append(sel)
    return pl.pallas_call(
        functools.partial(_band_attn_kernel, mode=mode),
        grid=(B, G, T // qb),
        in_specs=in_specs,
        out_specs=pl.BlockSpec((None, None, R, qb, Dh), lambda b, g, i: (b, g, 0, i, 0)),
        out_shape=jax.ShapeDtypeStruct((B, G, R, T, Dh), jnp.float32),
        scratch_shapes=[
            pltpu.VMEM((R * qb, 1), jnp.float32),
            pltpu.VMEM((R * qb, 1), jnp.float32),
            pltpu.VMEM((R * qb, Dh), jnp.float32),
        ],
        compiler_params=pltpu.CompilerParams(
            dimension_semantics=("parallel", "parallel", "arbitrary"),
            vmem_limit_bytes=VMEM_LIMIT_BYTES),
        name="band_attention_" + mode,
    )(*args)


def nsa_bias_tables(table, T, ncp):
    assert _bias_saturation_distance() <= Q_BLOCK + 1
    G, R = N_NSA_KV, NSA_GROUP
    c_end = jnp.arange(ncp) * CMP_STRIDE + CMP_BLOCK - 1
    dist_c = jnp.arange(T)[:, None] - c_end[None, :]
    cmp_bias = jnp.moveaxis(jnp.take(table, t5_bucket(dist_c), axis=0), -1, 0).reshape(G, R, T, ncp)
    tl = jnp.arange(Q_BLOCK)
    dist_t = jnp.arange(3)[:, None, None] * Q_BLOCK + tl[None, :, None] - tl[None, None, :]
    tile_bias = jnp.moveaxis(jnp.take(table, t5_bucket(dist_t), axis=0), -1, 1).reshape(3, G, R, Q_BLOCK, Q_BLOCK)
    return cmp_bias, tile_bias


PEER_ROUTE_TM = 256
PEER_HALF = PEER_D_KEY // 2


def _top_rows(s, order, count, payload=None):
    big = jnp.int32(2 ** 30)
    vals, picks = [], []
    for _ in range(count):
        m = jnp.max(s, axis=0, keepdims=True)
        o = jnp.min(jnp.where(s == m, order, big), axis=0, keepdims=True)
        hit = order == o
        vals.append(m)
        picks.append(o if payload is None else jnp.max(jnp.where(hit, payload, -1), axis=0, keepdims=True))
        s = jnp.where(hit, -jnp.inf, s)
    return vals, picks


def _peer_route_kernel(x_ref, g_ref, wq_ref, keys_ref, h_ref, idx_ref, gate_ref):
    tm = x_ref.shape[0]
    nk = keys_ref.shape[1]
    topk = PEER_TOPK
    x = x_ref[...]
    ms = jnp.mean(x * x, axis=-1, keepdims=True)
    h = (x * lax.rsqrt(ms + EPS) * g_ref[...]).astype(jnp.bfloat16)
    h_ref[...] = h
    key_iota = lax.broadcasted_iota(jnp.int32, (nk, tm), 0)

    def head_body(head, carry):
        tops = []
        for p in range(2):
            hp = head * 2 + p
            qhp = jnp.dot(h, wq_ref[hp], preferred_element_type=jnp.float32).astype(jnp.bfloat16)
            s = lax.dot_general(keys_ref[hp], qhp, (((1,), (1,)), ((), ())), preferred_element_type=jnp.float32)
            vals, picks = _top_rows(s, key_iota, topk)
            tops.append((jnp.concatenate(vals, axis=0), jnp.concatenate(picks, axis=0)))
        (s0, i0), (s1, i1) = tops
        cs, cpos, ce = [], [], []

        def add(rows_s, rows_pos, rows_e):
            cs.append(rows_s); cpos.append(rows_pos); ce.append(rows_e)

        jr16 = lax.broadcasted_iota(jnp.int32, (topk, tm), 0)
        add(s0[0:1] + s1, jr16, i0[0:1] * nk + i1)
        jr8 = lax.broadcasted_iota(jnp.int32, (8, tm), 0)
        for i in range(1, 8):
            nj = topk // (i + 1)
            vals = s0[i:i + 1] + s1[0:8]
            add(jnp.where(jr8 < nj, vals, -jnp.inf), i * topk + jr8, i0[i:i + 1] * nk + i1[0:8])
        add(s0[8:16] + s1[0:1], (8 + jr8) * topk, i0[8:16] * nk + i1[0:1])
        cand_s = jnp.concatenate(cs, axis=0)
        cand_pos = jnp.concatenate(cpos, axis=0)
        cand_e = jnp.concatenate(ce, axis=0)
        best_s, best_e = _top_rows(cand_s, cand_pos, topk, payload=cand_e)
        bs = jnp.concatenate(best_s, axis=0)
        ex = jnp.exp(bs - bs[0:1])
        gate = ex / jnp.sum(ex, axis=0, keepdims=True)
        row0 = pl.multiple_of(head * topk, topk)
        idx_ref[pl.ds(row0, topk), :] = jnp.concatenate(best_e, axis=0)
        gate_ref[pl.ds(row0, topk), :] = gate
        return carry

    lax.fori_loop(0, PEER_HEADS, head_body, 0)


def peer_route(x1, g, w_query, sub_keys):
    n, d = x1.shape
    assert PEER_TOPK == 16 and PEER_N_KEYS % 8 == 0
    tm = min(PEER_ROUTE_TM, n)
    hk = PEER_HEADS * PEER_TOPK
    wq = w_query.reshape(d, 2 * PEER_HEADS, PEER_HALF).transpose(1, 0, 2).astype(jnp.bfloat16)
    keys = sub_keys.reshape(2 * PEER_HEADS, PEER_N_KEYS, PEER_HALF).astype(jnp.bfloat16)
    return pl.pallas_call(
        _peer_route_kernel,
        grid=(n // tm,),
        in_specs=[
            pl.BlockSpec((tm, d), lambda i: (i, 0)),
            pl.BlockSpec((1, d), lambda i: (0, 0)),
            pl.BlockSpec((2 * PEER_HEADS, d, PEER_HALF), lambda i: (0, 0, 0)),
            pl.BlockSpec((2 * PEER_HEADS, PEER_N_KEYS, PEER_HALF), lambda i: (0, 0, 0)),
        ],
        out_specs=[
            pl.BlockSpec((tm, d), lambda i: (i, 0)),
            pl.BlockSpec((hk, tm), lambda i: (0, i)),
            pl.BlockSpec((hk, tm), lambda i: (0, i)),
        ],
        out_shape=[
            jax.ShapeDtypeStruct((n, d), jnp.bfloat16),
            jax.ShapeDtypeStruct((hk, n), jnp.int32),
            jax.ShapeDtypeStruct((hk, n), jnp.float32),
        ],
        compiler_params=pltpu.CompilerParams(
            dimension_semantics=("parallel",),
            vmem_limit_bytes=VMEM_LIMIT_BYTES),
        name="peer_route",
    )(x1, g.reshape(1, d), wq, keys)


PEER_TB = 8
PEER_SLOTS = 3
PEER_USES = PEER_HEADS * PEER_TOPK
PEER_ROWS = PEER_TB * PEER_USES
LANES = 128


def _peer_expert_kernel(idx0_ref, idx1_ref, idx2_ref, ht_ref, gate_ref, x_ref, uv_hbm, o_ref, buf, sem, act_t):
    i = pl.program_id(0)
    n = pl.num_programs(0)
    tb, d = x_ref.shape
    ns = d // LANES
    uses = PEER_USES

    def row_copy(idx_ref, slot, tok, j):
        src = uv_hbm.at[pl.ds(pl.multiple_of(idx_ref[tok * uses + j] * ns, ns), ns)]
        return pltpu.make_async_copy(src, buf.at[slot, tok, :, j, :], sem.at[slot])

    def slot_wait(slot):
        pltpu.make_async_copy(buf.at[slot], buf.at[slot], sem.at[slot]).wait()

    @pl.when(i == 0)
    def _():
        for tok in range(tb):
            def first(j, c):
                row_copy(idx0_ref, 0, tok, j).start()
                row_copy(idx1_ref, 1, tok, j).start()
                return c
            lax.fori_loop(0, uses, first, 0)

    slot = i % PEER_SLOTS
    slot_next = (i + 2) % PEER_SLOTS
    slot_wait(slot)

    half = uses // 2
    col0 = (i % (LANES // tb)) * tb
    lane = lax.broadcasted_iota(jnp.int32, (uses, LANES), 1)

    acts = jnp.zeros((uses, LANES), jnp.float32)
    for tok in range(tb):
        for j in range(half):
            row_copy(idx2_ref, slot_next, tok, j).start()
        u = jnp.concatenate(
            [lax.bitcast_convert_type(buf[slot, tok, s] & jnp.uint32(0xFFFF0000), jnp.float32).astype(jnp.bfloat16)
             for s in range(ns)], axis=1)
        r = jnp.dot(u, ht_ref[...], preferred_element_type=jnp.float32)
        acts = jnp.where(lane == col0 + tok, r, acts)
    act_t[...] = acts.T
    act = act_t[pl.ds(pl.multiple_of(col0, tb), tb), :]
    coef = (gate_ref[...] * jax.nn.gelu(act)).astype(jnp.bfloat16)

    for tok in range(tb):
        for j in range(half, uses):
            row_copy(idx2_ref, slot_next, tok, j).start()
        for s in range(ns):
            v = lax.bitcast_convert_type(buf[slot, tok, s] << 16, jnp.float32).astype(jnp.bfloat16)
            y = jnp.dot(coef, v, preferred_element_type=jnp.float32)
            cols = slice(s * LANES, (s + 1) * LANES)
            o_ref[tok:tok + 1, cols] = x_ref[tok:tok + 1, cols] + y[tok:tok + 1, :]

    @pl.when(i == n - 1)
    def _():
        slot_wait((i + 1) % PEER_SLOTS)
        slot_wait((i + 2) % PEER_SLOTS)


def peer_experts(x1, h_t, idx, gate, uv):
    n, d = x1.shape
    tb = PEER_TB
    nblk = n // tb
    assert n % LANES == 0 and LANES % tb == 0 and d % LANES == 0 and PEER_USES == LANES
    idx_flat = idx.reshape(n * PEER_USES)
    last = nblk - 1
    smem_spec = lambda ahead: pl.BlockSpec((PEER_ROWS,), lambda i: (jnp.minimum(i + ahead, last),),
                                           memory_space=pltpu.SMEM)
    return pl.pallas_call(
        _peer_expert_kernel,
        grid=(nblk,),
        in_specs=[
            smem_spec(0), smem_spec(1), smem_spec(2),
            pl.BlockSpec((d, LANES), lambda i: (0, i // (LANES // tb))),
            pl.BlockSpec((tb, PEER_USES), lambda i: (i, 0)),
            pl.BlockSpec((tb, d), lambda i: (i, 0)),
            pl.BlockSpec(memory_space=pl.ANY),
        ],
        out_specs=pl.BlockSpec((tb, d), lambda i: (i, 0)),
        out_shape=jax.ShapeDtypeStruct((n, d), jnp.float32),
        scratch_shapes=[
            pltpu.VMEM((PEER_SLOTS, tb, d // LANES, PEER_USES, LANES), jnp.uint32),
            pltpu.SemaphoreType.DMA((PEER_SLOTS,)),
            pltpu.VMEM((LANES, PEER_USES), jnp.float32),
        ],
        compiler_params=pltpu.CompilerParams(
            dimension_semantics=("arbitrary",),
            vmem_limit_bytes=VMEM_LIMIT_BYTES),
        name="peer_experts",
    )(idx_flat, idx_flat, idx_flat, h_t, gate, x1, uv)


def pack_expert_tables(expert_u, expert_v):
    e, d = expert_u.shape
    hi = lax.bitcast_convert_type(expert_u.astype(jnp.bfloat16), jnp.uint16).astype(jnp.uint32)
    lo = lax.bitcast_convert_type(expert_v.astype(jnp.bfloat16), jnp.uint16).astype(jnp.uint32)
    return ((hi << 16) | lo).reshape(e * (d // LANES), LANES)


def rms_norm(x, g):
    xf = x.astype(jnp.float32)
    y = xf * lax.rsqrt(jnp.mean(xf * xf, axis=-1, keepdims=True) + EPS)
    return (y * g.astype(jnp.float32)).astype(x.dtype)


def masked_softmax(s, valid):
    p = jax.nn.softmax(jnp.where(valid, s, NEG), axis=-1)
    return jnp.where(valid, p, 0.0)


def t5_bucket(dist):
    max_exact = N_BUCKETS // 2
    d = jnp.maximum(dist, 0)
    log_ratio = jnp.log(jnp.maximum(d, 1).astype(jnp.float32) / max_exact) / math.log(MAX_DISTANCE / max_exact)
    large = jnp.minimum(max_exact + (log_ratio * (N_BUCKETS - max_exact)).astype(jnp.int32), N_BUCKETS - 1)
    return jnp.where(d < max_exact, d, large)


def rel_bias_grid(table, dist):
    b = jnp.take(table, t5_bucket(dist), axis=0)
    return jnp.moveaxis(b, -1, 0).reshape(N_NSA_KV, NSA_GROUP, *dist.shape)


def stick_breaking_attention(q, k, v):
    B, H, T, Dh = q.shape
    scale = Dh ** -0.5
    kpos = jnp.arange(T)

    def block(i):
        qb = lax.dynamic_slice_in_dim(q, i * Q_BLOCK, Q_BLOCK, axis=2)
        qpos = i * Q_BLOCK + jnp.arange(Q_BLOCK)
        z = jnp.einsum('bhqd,bhkd->bhqk', qb, k, preferred_element_type=jnp.float32) * scale
        before = kpos[None, :] < qpos[:, None]
        log_keep = jnp.where(before, -jax.nn.softplus(z), 0.0)
        log_between = lax.cumsum(log_keep, axis=3, reverse=True) - log_keep
        a = jnp.where(before, jnp.exp(jax.nn.log_sigmoid(z) + log_between), 0.0)
        return jnp.einsum('bhqk,bhkd->bqhd', a.astype(v.dtype), v)

    out = lax.map(block, jnp.arange(T // Q_BLOCK))
    return jnp.moveaxis(out, 0, 1).reshape(B, T, H * Dh)


def compress_blocks(kv, pe, w1, w2):
    B, T, G, Dh = kv.shape
    nc = (T - CMP_BLOCK) // CMP_STRIDE + 1
    idx = jnp.arange(nc)[:, None] * CMP_STRIDE + jnp.arange(CMP_BLOCK)[None, :]
    blocks = kv[:, idx] + pe[None, None, :, None, :]
    flat = jnp.moveaxis(blocks, 3, 2).reshape(B, nc, G, CMP_BLOCK * Dh)
    return jax.nn.gelu(flat @ w1) @ w2


def block_overlap(nc, nb):
    c0 = np.arange(nc)[:, None] * CMP_STRIDE
    s0 = np.arange(nb)[None, :] * SEL_BLOCK
    ov = np.minimum(c0 + CMP_BLOCK, s0 + SEL_BLOCK) - np.maximum(c0, s0)
    return jnp.asarray(np.clip(ov, 0, None).astype(np.float32) / np.float32(CMP_BLOCK))


def compressed_attention(q, kc, vc, table):
    B, G, R, T, Dh = q.shape
    nc = kc.shape[2]
    nb = T // SEL_BLOCK
    scale = Dh ** -0.5
    c_end = jnp.arange(nc) * CMP_STRIDE + CMP_BLOCK - 1
    overlap = block_overlap(nc, nb)

    def block(i):
        qb = lax.dynamic_slice_in_dim(q, i * Q_BLOCK, Q_BLOCK, axis=3)
        qpos = i * Q_BLOCK + jnp.arange(Q_BLOCK)
        dist = qpos[:, None] - c_end[None, :]
        s = jnp.einsum('bgrqd,bgcd->bgrqc', qb, kc, preferred_element_type=jnp.float32) * scale + rel_bias_grid(table, dist)
        p = masked_softmax(s, dist >= 0)
        o = jnp.einsum('bgrqc,bgcd->bgrqd', p.astype(vc.dtype), vc)
        imp = jnp.einsum('bgrqc,cn->bgqn', p, overlap)
        return o, imp

    o, imp = lax.map(block, jnp.arange(T // Q_BLOCK))
    return (jnp.moveaxis(o, 0, 3).reshape(B, G, R, T, Dh), jnp.moveaxis(imp, 0, 2).reshape(B, G, T, nb))


def selected_attention(q, k, v, imp, table):
    B, G, R, T, Dh = q.shape
    nb = T // SEL_BLOCK
    n_sel = min(N_SELECT, nb)
    scale = Dh ** -0.5
    blk = jnp.arange(nb)
    cur = jnp.arange(T) // SEL_BLOCK
    forced = (blk[None, :] == 0) | (blk[None, :] == cur[:, None]) | (blk[None, :] == cur[:, None] - 1)
    future = blk[None, :] > cur[:, None]
    score = jnp.where(forced, BIG, jnp.where(future, NEG, imp))
    _, sel = lax.top_k(score, n_sel)
    kb = k.reshape(B, G, nb, SEL_BLOCK, Dh)
    vb = v.reshape(B, G, nb, SEL_BLOCK, Dh)
    gather = jax.vmap(jax.vmap(lambda blocks, ix: blocks[ix]))
    tab_gr = jnp.swapaxes(table.reshape(N_BUCKETS, G, R), 0, 1)
    g_index = jnp.arange(G)[None, :, None, None]
    n_keys = n_sel * SEL_BLOCK

    def chunk(i):
        qc = lax.dynamic_slice_in_dim(q, i * SEL_Q_CHUNK, SEL_Q_CHUNK, axis=3)
        ic = lax.dynamic_slice_in_dim(sel, i * SEL_Q_CHUNK, SEL_Q_CHUNK, axis=2)
        qpos = i * SEL_Q_CHUNK + jnp.arange(SEL_Q_CHUNK)
        kg = gather(kb, ic).reshape(B, G, SEL_Q_CHUNK, n_keys, Dh)
        vg = gather(vb, ic).reshape(B, G, SEL_Q_CHUNK, n_keys, Dh)
        kpos = (ic[..., None] * SEL_BLOCK + jnp.arange(SEL_BLOCK)).reshape(B, G, SEL_Q_CHUNK, n_keys)
        dist = qpos[None, None, :, None] - kpos
        bias = jnp.moveaxis(tab_gr[g_index, t5_bucket(dist)], -1, 2)
        s = jnp.einsum('bgrqd,bgqkd->bgrqk', qc, kg, preferred_element_type=jnp.float32) * scale + bias
        p = masked_softmax(s, (dist >= 0)[:, :, None])
        return jnp.einsum('bgrqk,bgqkd->bgrqd', p.astype(vg.dtype), vg)

    o = lax.map(chunk, jnp.arange(T // SEL_Q_CHUNK))
    return jnp.moveaxis(o, 0, 3).reshape(B, G, R, T, Dh)


def window_attention(q, k, v, table):
    B, G, R, T, Dh = q.shape
    scale = Dh ** -0.5
    span = WINDOW + Q_BLOCK
    kp = jnp.pad(k, ((0, 0), (0, 0), (WINDOW, 0), (0, 0)))
    vp = jnp.pad(v, ((0, 0), (0, 0), (WINDOW, 0), (0, 0)))

    def block(i):
        qb = lax.dynamic_slice_in_dim(q, i * Q_BLOCK, Q_BLOCK, axis=3)
        kb = lax.dynamic_slice_in_dim(kp, i * Q_BLOCK, span, axis=2)
        vb = lax.dynamic_slice_in_dim(vp, i * Q_BLOCK, span, axis=2)
        qpos = i * Q_BLOCK + jnp.arange(Q_BLOCK)
        kpos = i * Q_BLOCK - WINDOW + jnp.arange(span)
        dist = qpos[:, None] - kpos[None, :]
        valid = (dist >= 0) & (dist < WINDOW) & (kpos[None, :] >= 0)
        s = jnp.einsum('bgrqd,bgkd->bgrqk', qb, kb, preferred_element_type=jnp.float32) * scale + rel_bias_grid(table, dist)
        p = masked_softmax(s, valid)
        return jnp.einsum('bgrqk,bgkd->bgrqd', p.astype(vb.dtype), vb)

    o = lax.map(block, jnp.arange(T // Q_BLOCK))
    return jnp.moveaxis(o, 0, 3).reshape(B, G, R, T, Dh)


def peer_ffn(h, w_query, sub_keys, expert_u, expert_v):
    B, T, D = h.shape
    n_tok = B * T
    C = PEER_TOKEN_CHUNK
    half = PEER_D_KEY // 2

    def chunk(xc):
        qc = (xc @ w_query).reshape(C, PEER_HEADS, 2, half)
        s = jnp.einsum('chpd,hpnd->chpn', qc, sub_keys, preferred_element_type=jnp.float32)
        top_s, top_i = lax.top_k(s, PEER_TOPK)
        cand_s = (top_s[:, :, 0, :, None] + top_s[:, :, 1, None, :]).reshape(C, PEER_HEADS, PEER_TOPK * PEER_TOPK)
        cand_i = (top_i[:, :, 0, :, None] * PEER_N_KEYS + top_i[:, :, 1, None, :]).reshape(C, PEER_HEADS, PEER_TOPK * PEER_TOPK)
        best_s, best_pos = lax.top_k(cand_s, PEER_TOPK)
        expert_idx = jnp.take_along_axis(cand_i, best_pos, axis=-1)
        g = jax.nn.softmax(best_s, axis=-1)
        u = expert_u[expert_idx]
        vv = expert_v[expert_idx]
        act = jax.nn.gelu(jnp.einsum('chkd,cd->chk', u, xc, preferred_element_type=jnp.float32))
        return jnp.einsum('chk,chkd->cd', (g * act).astype(vv.dtype), vv)

    out = lax.map(chunk, h.reshape(n_tok // C, C, D))
    return out.reshape(B, T, D)


def kernel(x, norm1_g, w_in, cmp_pe, cmp_k_w1, cmp_k_w2, cmp_v_w1, cmp_v_w2, q_norm_g, k_norm_g, rel_bias_table, sb_out_g, nsa_out_g, w_out, norm2_g, peer_w_query, peer_sub_keys, peer_expert_u, peer_expert_v):
    B, T, D = x.shape
    G, R, Dh = N_NSA_KV, NSA_GROUP, HEAD_DIM
    l = 0
    x2 = x.reshape(B * T, D)

    n_pad = (-IN_COLS) % 640
    w_in_p = jnp.pad(w_in[l], ((0, 0), (0, n_pad))).astype(jnp.bfloat16)
    proj = norm_matmul(x2, norm1_g[l], w_in_p).reshape(B, T, IN_COLS + n_pad)

    o_sb = stick_breaking_attention_pallas(proj, N_SB_HEADS, 0, N_SB_HEADS, 2 * N_SB_HEADS)

    cuts = [int(c) - 3 * SB_WIDTH for c in np.cumsum(IN_SPLITS)[3:-1]]
    q_nsa, k_cmp, v_cmp, k_slc, v_slc, k_win, v_win, gate_logits = jnp.split(
        proj[:, :, 3 * SB_WIDTH:IN_COLS].astype(jnp.float32), cuts, axis=-1)

    kv_heads = lambda t: t.reshape(B, T, G, Dh)
    to_bgtd = lambda t: t.transpose(0, 2, 1, 3)
    q = rms_norm(q_nsa.reshape(B, T, G, R, Dh), q_norm_g[l]).transpose(0, 2, 3, 1, 4)
    kc = to_bgtd(rms_norm(compress_blocks(kv_heads(k_cmp), cmp_pe[l], cmp_k_w1[l], cmp_k_w2[l]), k_norm_g[l]))
    vc = to_bgtd(compress_blocks(kv_heads(v_cmp), cmp_pe[l], cmp_v_w1[l], cmp_v_w2[l]))
    bf16 = jnp.bfloat16
    qs = (q * (Dh ** -0.5)).astype(bf16)
    n_cmp = kc.shape[2]
    ncp = -(-n_cmp // 128) * 128
    pad_c = lambda t: jnp.pad(t, ((0, 0), (0, 0), (0, ncp - n_cmp), (0, 0))).astype(bf16)
    cmp_bias, tile_bias = nsa_bias_tables(rel_bias_table, T, ncp)
    o_cmp, sel_t = compressed_attention_select(qs, pad_c(kc), pad_c(vc), cmp_bias, n_cmp)
    sel = jnp.swapaxes(sel_t, 2, 3).astype(bf16)
    o_slc = band_attention(qs, to_bgtd(rms_norm(kv_heads(k_slc), k_norm_g[l])).astype(bf16),
                           to_bgtd(kv_heads(v_slc)).astype(bf16), tile_bias, sel)
    o_win = band_attention(qs, to_bgtd(rms_norm(kv_heads(k_win), k_norm_g[l])).astype(bf16),
                           to_bgtd(kv_heads(v_win)).astype(bf16), tile_bias)
    gates = jax.nn.sigmoid(gate_logits).reshape(B, T, G, R, 3).transpose(0, 2, 3, 1, 4)
    o_nsa = gates[..., 0:1] * o_cmp + gates[..., 1:2] * o_slc + gates[..., 2:3] * o_win
    o_nsa = o_nsa.transpose(0, 3, 1, 2, 4).reshape(B, T, NSA_WIDTH)

    mixed = jnp.concatenate([rms_norm(o_sb, sb_out_g[l]), rms_norm(o_nsa, nsa_out_g[l])], axis=-1)
    x1 = matmul_residual(mixed.reshape(B * T, MIX_WIDTH), w_out[l].astype(jnp.bfloat16), x2)

    h2, idx_t, gate_t = peer_route(x1, norm2_g[l], peer_w_query[l], peer_sub_keys[l])
    uv = pack_expert_tables(peer_expert_u[l], peer_expert_v[l])
    out = peer_experts(x1, h2.T, idx_t.T, gate_t.T, uv)
    return out.reshape(B, T, D)
```

```python
import functools
import math

import numpy as np
import jax
import jax.numpy as jnp
from jax import lax
from jax.experimental import pallas as pl
from jax.experimental.pallas import tpu as pltpu

D_MODEL = 2048
HEAD_DIM = 128
N_SB_HEADS = 8
N_NSA_HEADS = 8
N_NSA_KV = 2
NSA_GROUP = N_NSA_HEADS // N_NSA_KV
SB_WIDTH = N_SB_HEADS * HEAD_DIM
NSA_WIDTH = N_NSA_HEADS * HEAD_DIM
KV_WIDTH = N_NSA_KV * HEAD_DIM
MIX_WIDTH = SB_WIDTH + NSA_WIDTH
N_GATES = 3 * N_NSA_HEADS
IN_SPLITS = (SB_WIDTH, SB_WIDTH, SB_WIDTH, NSA_WIDTH, KV_WIDTH, KV_WIDTH, KV_WIDTH, KV_WIDTH, KV_WIDTH, KV_WIDTH, N_GATES)
IN_COLS = sum(IN_SPLITS)
Q_BLOCK = 128
CMP_BLOCK = 32
CMP_STRIDE = 16
CMP_HIDDEN = 256
SEL_BLOCK = 64
N_SELECT = 16
SEL_Q_CHUNK = 32
WINDOW = 512
N_BUCKETS = 32
MAX_DISTANCE = 128
PEER_HEADS = 8
PEER_N_KEYS = 128
PEER_D_KEY = 256
PEER_TOPK = 16
PEER_TOKEN_CHUNK = 128
EPS = 1e-6
NEG = -1e30
BIG = 1e30

VMEM_LIMIT_BYTES = 56 * 1024 * 1024


def _norm_matmul_kernel(x_ref, g_ref, w_ref, o_ref, h_ref):
    @pl.when(pl.program_id(1) == 0)
    def _():
        x = x_ref[...]
        ms = jnp.mean(x * x, axis=-1, keepdims=True)
        h_ref[...] = (x * lax.rsqrt(ms + EPS) * g_ref[...]).astype(jnp.bfloat16)

    o_ref[...] = jnp.dot(h_ref[...], w_ref[...], preferred_element_type=jnp.float32).astype(o_ref.dtype)


def norm_matmul(x, g, w, tm=1024, tn=640, out_dtype=jnp.bfloat16):
    m, k = x.shape
    n = w.shape[1]
    tm = min(tm, m)
    assert m % tm == 0 and n % tn == 0
    return pl.pallas_call(
        _norm_matmul_kernel,
        grid=(m // tm, n // tn),
        in_specs=[
            pl.BlockSpec((tm, k), lambda i, j: (i, 0)),
            pl.BlockSpec((1, k), lambda i, j: (0, 0)),
            pl.BlockSpec((k, tn), lambda i, j: (0, j)),
        ],
        out_specs=pl.BlockSpec((tm, tn), lambda i, j: (i, j)),
        out_shape=jax.ShapeDtypeStruct((m, n), out_dtype),
        scratch_shapes=[pltpu.VMEM((tm, k), jnp.bfloat16)],
        compiler_params=pltpu.CompilerParams(
            dimension_semantics=("parallel", "arbitrary"),
            vmem_limit_bytes=VMEM_LIMIT_BYTES),
        name="norm_matmul",
    )(x, g.reshape(1, k), w)


def _matmul_residual_kernel(a_ref, w_ref, r_ref, o_ref):
    o_ref[...] = r_ref[...] + jnp.dot(a_ref[...].astype(jnp.bfloat16), w_ref[...],
                                      preferred_element_type=jnp.float32)


def matmul_residual(a, w, r, tm=512, tn=512):
    m, k = a.shape
    n = w.shape[1]
    assert m % tm == 0 and n % tn == 0
    return pl.pallas_call(
        _matmul_residual_kernel,
        grid=(m // tm, n // tn),
        in_specs=[
            pl.BlockSpec((tm, k), lambda i, j: (i, 0)),
            pl.BlockSpec((k, tn), lambda i, j: (0, j)),
            pl.BlockSpec((tm, tn), lambda i, j: (i, j)),
        ],
        out_specs=pl.BlockSpec((tm, tn), lambda i, j: (i, j)),
        out_shape=jax.ShapeDtypeStruct((m, n), jnp.float32),
        compiler_params=pltpu.CompilerParams(
            dimension_semantics=("parallel", "arbitrary"),
            vmem_limit_bytes=VMEM_LIMIT_BYTES),
        name="matmul_residual",
    )(a, w, r)


SB_Q_TILE = 256
SB_K_TILE = 128
F32_EXP_UNDERFLOW = -104.0


def _sb_attn_kernel(q_ref, k_ref, v_ref, o_ref, c_ref, acc_ref):
    tq, dh = q_ref.shape
    tk = SB_K_TILE
    qi = pl.program_id(2)
    q = q_ref[...]
    scale = dh ** -0.5
    c_ref[...] = jnp.zeros(c_ref.shape, jnp.float32)
    acc_ref[...] = jnp.zeros(acc_ref.shape, jnp.float32)
    jj = lax.broadcasted_iota(jnp.int32, (2 * tk, 2 * tk), 0) % tk
    ss = lax.broadcasted_iota(jnp.int32, (2 * tk, 2 * tk), 1)
    suffix_and_total = jnp.where((jj > ss) | (ss >= tk), 1.0, 0.0).astype(jnp.bfloat16)
    row = lax.broadcasted_iota(jnp.int32, (tq, tk), 0)
    col = lax.broadcasted_iota(jnp.int32, (tq, tk), 1)

    def tile(kj, masked):
        start = pl.multiple_of(kj * tk, tk)
        k = k_ref[pl.ds(start, tk), :]
        v = v_ref[pl.ds(start, tk), :]
        z = lax.dot_general(q, k, (((1,), (1,)), ((), ())), preferred_element_type=jnp.float32) * scale
        t = jnp.log(1.0 + jnp.exp(-jnp.abs(z)))
        log_keep = -(jnp.maximum(z, 0.0) + t)
        log_beta = jnp.minimum(z, 0.0) - t
        if masked:
            before = (kj * tk + col) < (qi * tq + row)
            log_keep = jnp.where(before, log_keep, 0.0)
        hi = log_keep.astype(jnp.bfloat16)
        lo = (log_keep - hi.astype(jnp.float32)).astype(jnp.bfloat16)
        sums = jnp.dot(jnp.concatenate([hi, lo], axis=1), suffix_and_total, preferred_element_type=jnp.float32)
        c = c_ref[...]
        a = jnp.exp(log_beta + sums[:, :tk] + c)
        if masked:
            a = jnp.where(before, a, 0.0)
        acc_ref[...] += jnp.dot(a.astype(jnp.bfloat16), v, preferred_element_type=jnp.float32)
        c_new = c + sums[:, tk:]
        c_ref[...] = c_new
        return jnp.max(c_new)

    n_diag = tq // tk
    c_max = jnp.float32(0.0)
    for d in range(n_diag - 1, -1, -1):
        c_max = tile(qi * n_diag + d, True)

    def cond(carry):
        kj, c_max = carry
        return (kj >= 0) & (c_max > F32_EXP_UNDERFLOW)

    def body(carry):
        kj, _ = carry
        return kj - 1, tile(kj, False)

    lax.while_loop(cond, body, (qi * n_diag - 1, c_max))
    o_ref[...] = acc_ref[...]


def stick_breaking_attention_pallas(proj, n_heads, q_col, k_col, v_col):
    B, T, _ = proj.shape
    dh = HEAD_DIM
    tq = min(SB_Q_TILE, T)
    return pl.pallas_call(
        _sb_attn_kernel,
        grid=(B, n_heads, T // tq),
        in_specs=[
            pl.BlockSpec((None, tq, dh), lambda b, h, i: (b, i, q_col + h)),
            pl.BlockSpec((None, T, dh), lambda b, h, i: (b, 0, k_col + h)),
            pl.BlockSpec((None, T, dh), lambda b, h, i: (b, 0, v_col + h)),
        ],
        out_specs=pl.BlockSpec((None, tq, dh), lambda b, h, i: (b, i, h)),
        out_shape=jax.ShapeDtypeStruct((B, T, n_heads * dh), jnp.float32),
        scratch_shapes=[
            pltpu.VMEM((tq, SB_K_TILE), jnp.float32),
            pltpu.VMEM((tq, dh), jnp.float32),
        ],
        compiler_params=pltpu.CompilerParams(
            dimension_semantics=("parallel", "parallel", "arbitrary"),
            vmem_limit_bytes=VMEM_LIMIT_BYTES),
        name="sb_attention",
    )(proj, proj, proj)


def _bias_saturation_distance():
    max_exact = N_BUCKETS // 2
    return int(math.ceil(max_exact * (MAX_DISTANCE / max_exact) ** ((N_BUCKETS - 1 - max_exact) / (N_BUCKETS - max_exact))))


def _cmp_attn_kernel(q_ref, kc_ref, vc_ref, near_ref, far_ref, ov_ref, o_ref, selt_ref, *, n_cmp, n_sel):
    r, qb, dh = q_ref.shape
    ncp = kc_ref.shape[0]
    nb = ov_ref.shape[0]
    qi = pl.program_id(2)
    q = q_ref[...].reshape(r * qb, dh)
    s = lax.dot_general(q, kc_ref[...], (((1,), (1,)), ((), ())), preferred_element_type=jnp.float32)
    wn = near_ref.shape[-1]
    first = qi * (qb // CMP_STRIDE) - qb // CMP_STRIDE
    place = jnp.where(lax.broadcasted_iota(jnp.int32, (wn, ncp), 1)
                      == first + lax.broadcasted_iota(jnp.int32, (wn, ncp), 0), 1.0, 0.0).astype(jnp.bfloat16)
    near = near_ref[...].reshape(r * qb, wn)
    near_hi = near.astype(jnp.bfloat16)
    near_lo = (near - near_hi.astype(jnp.float32)).astype(jnp.bfloat16)
    s = s + (jnp.dot(near_hi, place, preferred_element_type=jnp.float32)
             + jnp.dot(near_lo, place, preferred_element_type=jnp.float32))
    s = s.reshape(r, qb, ncp) + far_ref[...]
    tpos = qi * qb + lax.broadcasted_iota(jnp.int32, (qb, ncp), 0)
    c = lax.broadcasted_iota(jnp.int32, (qb, ncp), 1)
    valid = ((tpos - (c * CMP_STRIDE + CMP_BLOCK - 1)) >= 0) & (c < n_cmp)
    s = jnp.where(valid[None], s, NEG)
    m = jnp.max(s, axis=-1, keepdims=True)
    e = jnp.where(valid[None], jnp.exp(s - m), 0.0)
    l = jnp.sum(e, axis=-1, keepdims=True)
    p = e * jnp.where(l > 0.0, 1.0 / l, 0.0)
    o = jnp.dot(p.reshape(r * qb, ncp).astype(jnp.bfloat16), vc_ref[...], preferred_element_type=jnp.float32)
    o_ref[...] = o.reshape(r, qb, dh)

    psum = jnp.sum(p, axis=0)
    p_hi = psum.astype(jnp.bfloat16)
    p_lo = (psum - p_hi.astype(jnp.float32)).astype(jnp.bfloat16)
    nt = (((1,), (1,)), ((), ()))
    imp_t = (lax.dot_general(ov_ref[...], p_hi, nt, preferred_element_type=jnp.float32)
             + lax.dot_general(ov_ref[...], p_lo, nt, preferred_element_type=jnp.float32))

    blk = lax.broadcasted_iota(jnp.int32, (nb, qb), 0)
    cur = (qi * qb + lax.broadcasted_iota(jnp.int32, (nb, qb), 1)) // SEL_BLOCK
    forced = (blk == 0) | (blk == cur) | (blk == cur - 1)
    score = jnp.where(forced, BIG, jnp.where(blk > cur, NEG, imp_t))
    rank = jnp.zeros((nb, qb), jnp.float32)
    for mblk in range(nb):
        row = score[mblk:mblk + 1, :]
        ahead = (row > score) | ((row == score) & (blk > mblk))
        rank = rank + jnp.where(ahead, 1.0, 0.0)
    selt_ref[...] = jnp.where(rank < float(n_sel), 1.0, 0.0)


def compressed_attention_select(q, kc, vc, near_bias, far_bias, n_cmp):
    B, G, R, T, Dh = q.shape
    ncp = kc.shape[2]
    nb = T // SEL_BLOCK
    n_sel = min(N_SELECT, nb)
    c0 = np.arange(ncp)[None, :] * CMP_STRIDE
    s0 = np.arange(nb)[:, None] * SEL_BLOCK
    ov = np.clip(np.minimum(c0 + CMP_BLOCK, s0 + SEL_BLOCK) - np.maximum(c0, s0), 0, None).astype(np.float32) / np.float32(CMP_BLOCK)
    ov = np.where(np.arange(ncp)[None, :] < n_cmp, ov, 0.0)
    ov_t = jnp.asarray(ov, dtype=jnp.bfloat16)
    qb = Q_BLOCK
    return pl.pallas_call(
        functools.partial(_cmp_attn_kernel, n_cmp=n_cmp, n_sel=n_sel),
        grid=(B, G, T // qb),
        in_specs=[
            pl.BlockSpec((None, None, R, qb, Dh), lambda b, g, i: (b, g, 0, i, 0)),
            pl.BlockSpec((None, None, ncp, Dh), lambda b, g, i: (b, g, 0, 0)),
            pl.BlockSpec((None, None, ncp, Dh), lambda b, g, i: (b, g, 0, 0)),
            pl.BlockSpec((None, R, qb, LANES), lambda b, g, i: (g, 0, 0, 0)),
            pl.BlockSpec((None, R, 1, ncp), lambda b, g, i: (g, 0, 0, 0)),
            pl.BlockSpec((nb, ncp), lambda b, g, i: (0, 0)),
        ],
        out_specs=[
            pl.BlockSpec((None, None, R, qb, Dh), lambda b, g, i: (b, g, 0, i, 0)),
            pl.BlockSpec((None, None, nb, qb), lambda b, g, i: (b, g, 0, i)),
        ],
        out_shape=[
            jax.ShapeDtypeStruct((B, G, R, T, Dh), jnp.float32),
            jax.ShapeDtypeStruct((B, G, nb, T), jnp.float32),
        ],
        compiler_params=pltpu.CompilerParams(
            dimension_semantics=("parallel", "parallel", "arbitrary"),
            vmem_limit_bytes=VMEM_LIMIT_BYTES),
        name="cmp_attention_select",
    )(q, kc, vc, near_bias, far_bias, ov_t)


BAND_SLC_TILES_PER_STEP = 4


def _band_attn_kernel(*refs, mode):
    if mode == "slc":
        q_ref, k_ref, v_ref, bias_ref, sel_ref, o_ref, m_ref, l_ref, acc_ref = refs
    else:
        q_ref, k_ref, v_ref, bias_ref, o_ref, m_ref, l_ref, acc_ref = refs
    r, qb, dh = q_ref.shape
    qi = pl.program_id(2)
    q = q_ref[...].reshape(r * qb, dh)
    m_ref[...] = jnp.full(m_ref.shape, -jnp.inf, jnp.float32)
    l_ref[...] = jnp.zeros(l_ref.shape, jnp.float32)
    acc_ref[...] = jnp.zeros(acc_ref.shape, jnp.float32)
    row = lax.broadcasted_iota(jnp.int32, (qb, qb), 0)
    col = lax.broadcasted_iota(jnp.int32, (qb, qb), 1)
    if mode == "slc":
        nb = sel_ref.shape[1]
        sel = sel_ref[...]
        blk_of_col = lax.broadcasted_iota(jnp.int32, (nb, qb), 1) // SEL_BLOCK
        blk_row = lax.broadcasted_iota(jnp.int32, (nb, qb), 0)
        n_tiles = qi + 1
        kt = BAND_SLC_TILES_PER_STEP
    else:
        n_tiles = jnp.minimum(qi, WINDOW // qb) + 1
        kt = WINDOW // qb + 1
    n_steps = (n_tiles + kt - 1) // kt

    def body(step, carry):
        s_parts, v_parts = [], []
        for u in range(kt):
            off = step * kt + u
            kj = qi - off
            dead = jnp.where(off < n_tiles, 0, 1 << 20)
            start = pl.multiple_of(jnp.maximum(kj, 0) * qb, qb)
            k = k_ref[pl.ds(start, qb), :]
            s = lax.dot_general(q, k, (((1,), (1,)), ((), ())), preferred_element_type=jnp.float32)
            dist = off * qb + row - col
            if mode == "slc":
                expand = jnp.where(blk_row == blk_of_col + kj * (qb // SEL_BLOCK), 1.0, 0.0).astype(jnp.bfloat16)
                picked = jnp.dot(sel, expand, preferred_element_type=jnp.float32)
                valid = (picked > 0.5) & (dist - dead >= 0)
            else:
                valid = (dist >= 0) & (dist + dead < WINDOW)
            addmask = jnp.where(valid, 0.0, NEG)
            s = s.reshape(r, qb, qb) + (bias_ref[jnp.minimum(off, 2)] + addmask[None])
            s_parts.append(s.reshape(r * qb, qb))
            v_parts.append(v_ref[pl.ds(start, qb), :])
        s = jnp.concatenate(s_parts, axis=1)
        v = jnp.concatenate(v_parts, axis=0)
        m_prev = m_ref[...]
        m_new = jnp.maximum(m_prev, jnp.max(s, axis=-1, keepdims=True))
        alpha = jnp.exp(m_prev - m_new)
        p = jnp.exp(s - m_new)
        l_ref[...] = alpha * l_ref[...] + jnp.sum(p, axis=-1, keepdims=True)
        acc_ref[...] = alpha * acc_ref[...] + jnp.dot(p.astype(jnp.bfloat16), v, preferred_element_type=jnp.float32)
        m_ref[...] = m_new
        return carry

    lax.fori_loop(0, n_steps, body, 0)
    o_ref[...] = (acc_ref[...] / l_ref[...]).reshape(r, qb, dh)


def band_attention(q, k, v, tile_bias, sel=None):
    B, G, R, T, Dh = q.shape
    qb = Q_BLOCK
    mode = "win" if sel is None else "slc"
    in_specs = [
        pl.BlockSpec((None, None, R, qb, Dh), lambda b, g, i: (b, g, 0, i, 0)),
        pl.BlockSpec((None, None, T, Dh), lambda b, g, i: (b, g, 0, 0)),
        pl.BlockSpec((None, None, T, Dh), lambda b, g, i: (b, g, 0, 0)),
        pl.BlockSpec((3, None, R, qb, qb), lambda b, g, i: (0, g, 0, 0, 0)),
    ]
    args = [q, k, v, tile_bias]
    if sel is not None:
        nb = sel.shape[-1]
        in_specs.append(pl.BlockSpec((None, None, qb, nb), lambda b, g, i: (b, g, i, 0)))
        args.append(sel)
    return pl.pallas_call(
        functools.partial(_band_attn_kernel, mode=mode),
        grid=(B, G, T // qb),
        in_specs=in_specs,
        out_specs=pl.BlockSpec((None, None, R, qb, Dh), lambda b, g, i: (b, g, 0, i, 0)),
        out_shape=jax.ShapeDtypeStruct((B, G, R, T, Dh), jnp.float32),
        scratch_shapes=[
            pltpu.VMEM((R * qb, 1), jnp.float32),
            pltpu.VMEM((R * qb, 1), jnp.float32),
            pltpu.VMEM((R * qb, Dh), jnp.float32),
        ],
        compiler_params=pltpu.CompilerParams(
            dimension_semantics=("parallel", "parallel", "arbitrary"),
            vmem_limit_bytes=VMEM_LIMIT_BYTES),
        name="band_attention_" + mode,
    )(*args)


def nsa_bias_tables(table, T, ncp):
    sat = _bias_saturation_distance()
    assert sat <= Q_BLOCK + 1
    G, R = N_NSA_KV, NSA_GROUP
    back = Q_BLOCK // CMP_STRIDE
    width = 2 * back
    assert (back + 1) * CMP_STRIDE - (CMP_BLOCK - 1) >= sat and width <= LANES
    assert (Q_BLOCK - 1) - back * CMP_STRIDE - (CMP_BLOCK - 1) < 0
    tl = jnp.arange(Q_BLOCK)
    far_row = table[N_BUCKETS - 1]
    dist_n = tl[:, None] - (jnp.arange(width)[None, :] - back) * CMP_STRIDE - (CMP_BLOCK - 1)
    near = jnp.take(table, t5_bucket(dist_n), axis=0) - far_row
    near = jnp.pad(jnp.moveaxis(near, -1, 0), ((0, 0), (0, 0), (0, LANES - width))).reshape(G, R, Q_BLOCK, LANES)
    far = jnp.broadcast_to(far_row.reshape(G, R, 1, 1), (G, R, 1, ncp))
    dist_t = jnp.arange(3)[:, None, None] * Q_BLOCK + tl[None, :, None] - tl[None, None, :]
    tile_bias = jnp.moveaxis(jnp.take(table, t5_bucket(dist_t), axis=0), -1, 1).reshape(3, G, R, Q_BLOCK, Q_BLOCK)
    return near, far, tile_bias


PEER_ROUTE_TM = 256
PEER_HALF = PEER_D_KEY // 2


def _top_rows(s, order, count, payload=None):
    vals, picks = [], []
    for _ in range(count):
        m = jnp.max(s, axis=0, keepdims=True)
        o = jnp.min(jnp.where(s == m, order, jnp.inf), axis=0, keepdims=True)
        hit = order == o
        vals.append(m)
        picks.append(o if payload is None else jnp.max(jnp.where(hit, payload, -1.0), axis=0, keepdims=True))
        s = jnp.where(hit, -jnp.inf, s)
    return vals, picks


def _peer_route_kernel(x_ref, g_ref, wq_ref, keys_ref, h_ref, idx_ref, gate_ref):
    tm = x_ref.shape[0]
    nk = keys_ref.shape[1]
    topk = PEER_TOPK
    x = x_ref[...]
    ms = jnp.mean(x * x, axis=-1, keepdims=True)
    h = (x * lax.rsqrt(ms + EPS) * g_ref[...]).astype(jnp.bfloat16)
    h_ref[...] = h
    key_iota = lax.broadcasted_iota(jnp.int32, (nk, tm), 0).astype(jnp.float32)

    def head_body(head, carry):
        tops = []
        for p in range(2):
            hp = head * 2 + p
            qhp = jnp.dot(h, wq_ref[hp], preferred_element_type=jnp.float32).astype(jnp.bfloat16)
            s = lax.dot_general(keys_ref[hp], qhp, (((1,), (1,)), ((), ())), preferred_element_type=jnp.float32)
            vals, picks = _top_rows(s, key_iota, topk)
            tops.append((jnp.concatenate(vals, axis=0), jnp.concatenate(picks, axis=0)))
        (s0, i0), (s1, i1) = tops
        cs, cpos, ce = [], [], []

        def add(rows_s, rows_pos, rows_e):
            cs.append(rows_s); cpos.append(rows_pos); ce.append(rows_e)

        jr16 = lax.broadcasted_iota(jnp.int32, (topk, tm), 0).astype(jnp.float32)
        add(s0[0:1] + s1, jr16, i0[0:1] * nk + i1)
        jr8 = lax.broadcasted_iota(jnp.int32, (8, tm), 0).astype(jnp.float32)
        for i in range(1, 8):
            nj = topk // (i + 1)
            vals = s0[i:i + 1] + s1[0:8]
            add(jnp.where(jr8 < nj, vals, -jnp.inf), i * topk + jr8, i0[i:i + 1] * nk + i1[0:8])
        add(s0[8:16] + s1[0:1], (8 + jr8) * topk, i0[8:16] * nk + i1[0:1])
        cand_s = jnp.concatenate(cs, axis=0)
        cand_pos = jnp.concatenate(cpos, axis=0)
        cand_e = jnp.concatenate(ce, axis=0)
        best_s, best_e = _top_rows(cand_s, cand_pos, topk, payload=cand_e)
        bs = jnp.concatenate(best_s, axis=0)
        ex = jnp.exp(bs - bs[0:1])
        gate = ex / jnp.sum(ex, axis=0, keepdims=True)
        row0 = pl.multiple_of(head * topk, topk)
        idx_ref[pl.ds(row0, topk), :] = jnp.concatenate(best_e, axis=0).astype(jnp.int32)
        gate_ref[pl.ds(row0, topk), :] = gate
        return carry

    lax.fori_loop(0, PEER_HEADS, head_body, 0)


def peer_route(x1, g, w_query, sub_keys):
    n, d = x1.shape
    assert PEER_TOPK == 16 and PEER_N_KEYS % 8 == 0
    tm = min(PEER_ROUTE_TM, n)
    hk = PEER_HEADS * PEER_TOPK
    wq = w_query.reshape(d, 2 * PEER_HEADS, PEER_HALF).transpose(1, 0, 2).astype(jnp.bfloat16)
    keys = sub_keys.reshape(2 * PEER_HEADS, PEER_N_KEYS, PEER_HALF).astype(jnp.bfloat16)
    return pl.pallas_call(
        _peer_route_kernel,
        grid=(n // tm,),
        in_specs=[
            pl.BlockSpec((tm, d), lambda i: (i, 0)),
            pl.BlockSpec((1, d), lambda i: (0, 0)),
            pl.BlockSpec((2 * PEER_HEADS, d, PEER_HALF), lambda i: (0, 0, 0)),
            pl.BlockSpec((2 * PEER_HEADS, PEER_N_KEYS, PEER_HALF), lambda i: (0, 0, 0)),
        ],
        out_specs=[
            pl.BlockSpec((tm, d), lambda i: (i, 0)),
            pl.BlockSpec((hk, tm), lambda i: (0, i)),
            pl.BlockSpec((hk, tm), lambda i: (0, i)),
        ],
        out_shape=[
            jax.ShapeDtypeStruct((n, d), jnp.bfloat16),
            jax.ShapeDtypeStruct((hk, n), jnp.int32),
            jax.ShapeDtypeStruct((hk, n), jnp.float32),
        ],
        compiler_params=pltpu.CompilerParams(
            dimension_semantics=("parallel",),
            vmem_limit_bytes=VMEM_LIMIT_BYTES),
        name="peer_route",
    )(x1, g.reshape(1, d), wq, keys)


PEER_TB = 8
PEER_SLOTS = 3
PEER_USES = PEER_HEADS * PEER_TOPK
PEER_ROWS = PEER_TB * PEER_USES
LANES = 128


def _peer_expert_kernel(idx0_ref, idx1_ref, idx2_ref, ht_ref, gate_ref, x_ref, uv_hbm, o_ref, buf, sem, act_t):
    i = pl.program_id(0)
    n = pl.num_programs(0)
    tb, d = x_ref.shape
    ns = d // LANES
    uses = PEER_USES

    def row_copy(idx_ref, slot, tok, j):
        src = uv_hbm.at[pl.ds(pl.multiple_of(idx_ref[tok * uses + j] * ns, ns), ns)]
        return pltpu.make_async_copy(src, buf.at[slot, tok, :, j, :], sem.at[slot])

    def slot_wait(slot):
        pltpu.make_async_copy(buf.at[slot], buf.at[slot], sem.at[slot]).wait()

    @pl.when(i == 0)
    def _():
        for tok in range(tb):
            def first(j, c):
                row_copy(idx0_ref, 0, tok, j).start()
                row_copy(idx1_ref, 1, tok, j).start()
                return c
            lax.fori_loop(0, uses, first, 0)

    slot = i % PEER_SLOTS
    slot_next = (i + 2) % PEER_SLOTS
    slot_wait(slot)

    half = uses // 2
    col0 = (i % (LANES // tb)) * tb
    lane = lax.broadcasted_iota(jnp.int32, (uses, LANES), 1)

    acts = jnp.zeros((uses, LANES), jnp.float32)
    for tok in range(tb):
        for j in range(half):
            row_copy(idx2_ref, slot_next, tok, j).start()
        u = jnp.concatenate(
            [lax.bitcast_convert_type(buf[slot, tok, s] & jnp.uint32(0xFFFF0000), jnp.float32).astype(jnp.bfloat16)
             for s in range(ns)], axis=1)
        r = jnp.dot(u, ht_ref[...], preferred_element_type=jnp.float32)
        acts = jnp.where(lane == col0 + tok, r, acts)
    act_t[...] = acts.T
    act = act_t[pl.ds(pl.multiple_of(col0, tb), tb), :]
    coef = (gate_ref[...] * jax.nn.gelu(act)).astype(jnp.bfloat16)

    for tok in range(tb):
        for j in range(half, uses):
            row_copy(idx2_ref, slot_next, tok, j).start()
        for s in range(ns):
            v = lax.bitcast_convert_type(buf[slot, tok, s] << 16, jnp.float32).astype(jnp.bfloat16)
            y = jnp.dot(coef, v, preferred_element_type=jnp.float32)
            cols = slice(s * LANES, (s + 1) * LANES)
            o_ref[tok:tok + 1, cols] = x_ref[tok:tok + 1, cols] + y[tok:tok + 1, :]

    @pl.when(i == n - 1)
    def _():
        slot_wait((i + 1) % PEER_SLOTS)
        slot_wait((i + 2) % PEER_SLOTS)


def peer_experts(x1, h_t, idx, gate, uv):
    n, d = x1.shape
    tb = PEER_TB
    nblk = n // tb
    assert n % LANES == 0 and LANES % tb == 0 and d % LANES == 0 and PEER_USES == LANES
    idx_flat = idx.reshape(n * PEER_USES)
    last = nblk - 1
    smem_spec = lambda ahead: pl.BlockSpec((PEER_ROWS,), lambda i: (jnp.minimum(i + ahead, last),),
                                           memory_space=pltpu.SMEM)
    return pl.pallas_call(
        _peer_expert_kernel,
        grid=(nblk,),
        in_specs=[
            smem_spec(0), smem_spec(1), smem_spec(2),
            pl.BlockSpec((d, LANES), lambda i: (0, i // (LANES // tb))),
            pl.BlockSpec((tb, PEER_USES), lambda i: (i, 0)),
            pl.BlockSpec((tb, d), lambda i: (i, 0)),
            pl.BlockSpec(memory_space=pl.ANY),
        ],
        out_specs=pl.BlockSpec((tb, d), lambda i: (i, 0)),
        out_shape=jax.ShapeDtypeStruct((n, d), jnp.float32),
        scratch_shapes=[
            pltpu.VMEM((PEER_SLOTS, tb, d // LANES, PEER_USES, LANES), jnp.uint32),
            pltpu.SemaphoreType.DMA((PEER_SLOTS,)),
            pltpu.VMEM((LANES, PEER_USES), jnp.float32),
        ],
        compiler_params=pltpu.CompilerParams(
            dimension_semantics=("arbitrary",),
            vmem_limit_bytes=VMEM_LIMIT_BYTES),
        name="peer_experts",
    )(idx_flat, idx_flat, idx_flat, h_t, gate, x1, uv)


def pack_expert_tables(expert_u, expert_v):
    e, d = expert_u.shape
    hi = lax.bitcast_convert_type(expert_u.astype(jnp.bfloat16), jnp.uint16).astype(jnp.uint32)
    lo = lax.bitcast_convert_type(expert_v.astype(jnp.bfloat16), jnp.uint16).astype(jnp.uint32)
    return ((hi << 16) | lo).reshape(e * (d // LANES), LANES)


def rms_norm(x, g):
    xf = x.astype(jnp.float32)
    y = xf * lax.rsqrt(jnp.mean(xf * xf, axis=-1, keepdims=True) + EPS)
    return (y * g.astype(jnp.float32)).astype(x.dtype)


def masked_softmax(s, valid):
    p = jax.nn.softmax(jnp.where(valid, s, NEG), axis=-1)
    return jnp.where(valid, p, 0.0)


def t5_bucket(dist):
    max_exact = N_BUCKETS // 2
    d = jnp.maximum(dist, 0)
    log_ratio = jnp.log(jnp.maximum(d, 1).astype(jnp.float32) / max_exact) / math.log(MAX_DISTANCE / max_exact)
    large = jnp.minimum(max_exact + (log_ratio * (N_BUCKETS - max_exact)).astype(jnp.int32), N_BUCKETS - 1)
    return jnp.where(d < max_exact, d, large)


def rel_bias_grid(table, dist):
    b = jnp.take(table, t5_bucket(dist), axis=0)
    return jnp.moveaxis(b, -1, 0).reshape(N_NSA_KV, NSA_GROUP, *dist.shape)


def stick_breaking_attention(q, k, v):
    B, H, T, Dh = q.shape
    scale = Dh ** -0.5
    kpos = jnp.arange(T)

    def block(i):
        qb = lax.dynamic_slice_in_dim(q, i * Q_BLOCK, Q_BLOCK, axis=2)
        qpos = i * Q_BLOCK + jnp.arange(Q_BLOCK)
        z = jnp.einsum('bhqd,bhkd->bhqk', qb, k, preferred_element_type=jnp.float32) * scale
        before = kpos[None, :] < qpos[:, None]
        log_keep = jnp.where(before, -jax.nn.softplus(z), 0.0)
        log_between = lax.cumsum(log_keep, axis=3, reverse=True) - log_keep
        a = jnp.where(before, jnp.exp(jax.nn.log_sigmoid(z) + log_between), 0.0)
        return jnp.einsum('bhqk,bhkd->bqhd', a.astype(v.dtype), v)

    out = lax.map(block, jnp.arange(T // Q_BLOCK))
    return jnp.moveaxis(out, 0, 1).reshape(B, T, H * Dh)


def compress_blocks(kv, pe, w1, w2):
    B, T, G, Dh = kv.shape
    nc = (T - CMP_BLOCK) // CMP_STRIDE + 1
    idx = jnp.arange(nc)[:, None] * CMP_STRIDE + jnp.arange(CMP_BLOCK)[None, :]
    blocks = kv[:, idx] + pe[None, None, :, None, :]
    flat = jnp.moveaxis(blocks, 3, 2).reshape(B, nc, G, CMP_BLOCK * Dh)
    return jax.nn.gelu(flat @ w1) @ w2


def block_overlap(nc, nb):
    c0 = np.arange(nc)[:, None] * CMP_STRIDE
    s0 = np.arange(nb)[None, :] * SEL_BLOCK
    ov = np.minimum(c0 + CMP_BLOCK, s0 + SEL_BLOCK) - np.maximum(c0, s0)
    return jnp.asarray(np.clip(ov, 0, None).astype(np.float32) / np.float32(CMP_BLOCK))


def compressed_attention(q, kc, vc, table):
    B, G, R, T, Dh = q.shape
    nc = kc.shape[2]
    nb = T // SEL_BLOCK
    scale = Dh ** -0.5
    c_end = jnp.arange(nc) * CMP_STRIDE + CMP_BLOCK - 1
    overlap = block_overlap(nc, nb)

    def block(i):
        qb = lax.dynamic_slice_in_dim(q, i * Q_BLOCK, Q_BLOCK, axis=3)
        qpos = i * Q_BLOCK + jnp.arange(Q_BLOCK)
        dist = qpos[:, None] - c_end[None, :]
        s = jnp.einsum('bgrqd,bgcd->bgrqc', qb, kc, preferred_element_type=jnp.float32) * scale + rel_bias_grid(table, dist)
        p = masked_softmax(s, dist >= 0)
        o = jnp.einsum('bgrqc,bgcd->bgrqd', p.astype(vc.dtype), vc)
        imp = jnp.einsum('bgrqc,cn->bgqn', p, overlap)
        return o, imp

    o, imp = lax.map(block, jnp.arange(T // Q_BLOCK))
    return (jnp.moveaxis(o, 0, 3).reshape(B, G, R, T, Dh), jnp.moveaxis(imp, 0, 2).reshape(B, G, T, nb))


def selected_attention(q, k, v, imp, table):
    B, G, R, T, Dh = q.shape
    nb = T // SEL_BLOCK
    n_sel = min(N_SELECT, nb)
    scale = Dh ** -0.5
    blk = jnp.arange(nb)
    cur = jnp.arange(T) // SEL_BLOCK
    forced = (blk[None, :] == 0) | (blk[None, :] == cur[:, None]) | (blk[None, :] == cur[:, None] - 1)
    future = blk[None, :] > cur[:, None]
    score = jnp.where(forced, BIG, jnp.where(future, NEG, imp))
    _, sel = lax.top_k(score, n_sel)
    kb = k.reshape(B, G, nb, SEL_BLOCK, Dh)
    vb = v.reshape(B, G, nb, SEL_BLOCK, Dh)
    gather = jax.vmap(jax.vmap(lambda blocks, ix: blocks[ix]))
    tab_gr = jnp.swapaxes(table.reshape(N_BUCKETS, G, R), 0, 1)
    g_index = jnp.arange(G)[None, :, None, None]
    n_keys = n_sel * SEL_BLOCK

    def chunk(i):
        qc = lax.dynamic_slice_in_dim(q, i * SEL_Q_CHUNK, SEL_Q_CHUNK, axis=3)
        ic = lax.dynamic_slice_in_dim(sel, i * SEL_Q_CHUNK, SEL_Q_CHUNK, axis=2)
        qpos = i * SEL_Q_CHUNK + jnp.arange(SEL_Q_CHUNK)
        kg = gather(kb, ic).reshape(B, G, SEL_Q_CHUNK, n_keys, Dh)
        vg = gather(vb, ic).reshape(B, G, SEL_Q_CHUNK, n_keys, Dh)
        kpos = (ic[..., None] * SEL_BLOCK + jnp.arange(SEL_BLOCK)).reshape(B, G, SEL_Q_CHUNK, n_keys)
        dist = qpos[None, None, :, None] - kpos
        bias = jnp.moveaxis(tab_gr[g_index, t5_bucket(dist)], -1, 2)
        s = jnp.einsum('bgrqd,bgqkd->bgrqk', qc, kg, preferred_element_type=jnp.float32) * scale + bias
        p = masked_softmax(s, (dist >= 0)[:, :, None])
        return jnp.einsum('bgrqk,bgqkd->bgrqd', p.astype(vg.dtype), vg)

    o = lax.map(chunk, jnp.arange(T // SEL_Q_CHUNK))
    return jnp.moveaxis(o, 0, 3).reshape(B, G, R, T, Dh)


def window_attention(q, k, v, table):
    B, G, R, T, Dh = q.shape
    scale = Dh ** -0.5
    span = WINDOW + Q_BLOCK
    kp = jnp.pad(k, ((0, 0), (0, 0), (WINDOW, 0), (0, 0)))
    vp = jnp.pad(v, ((0, 0), (0, 0), (WINDOW, 0), (0, 0)))

    def block(i):
        qb = lax.dynamic_slice_in_dim(q, i * Q_BLOCK, Q_BLOCK, axis=3)
        kb = lax.dynamic_slice_in_dim(kp, i * Q_BLOCK, span, axis=2)
        vb = lax.dynamic_slice_in_dim(vp, i * Q_BLOCK, span, axis=2)
        qpos = i * Q_BLOCK + jnp.arange(Q_BLOCK)
        kpos = i * Q_BLOCK - WINDOW + jnp.arange(span)
        dist = qpos[:, None] - kpos[None, :]
        valid = (dist >= 0) & (dist < WINDOW) & (kpos[None, :] >= 0)
        s = jnp.einsum('bgrqd,bgkd->bgrqk', qb, kb, preferred_element_type=jnp.float32) * scale + rel_bias_grid(table, dist)
        p = masked_softmax(s, valid)
        return jnp.einsum('bgrqk,bgkd->bgrqd', p.astype(vb.dtype), vb)

    o = lax.map(block, jnp.arange(T // Q_BLOCK))
    return jnp.moveaxis(o, 0, 3).reshape(B, G, R, T, Dh)


def peer_ffn(h, w_query, sub_keys, expert_u, expert_v):
    B, T, D = h.shape
    n_tok = B * T
    C = PEER_TOKEN_CHUNK
    half = PEER_D_KEY // 2

    def chunk(xc):
        qc = (xc @ w_query).reshape(C, PEER_HEADS, 2, half)
        s = jnp.einsum('chpd,hpnd->chpn', qc, sub_keys, preferred_element_type=jnp.float32)
        top_s, top_i = lax.top_k(s, PEER_TOPK)
        cand_s = (top_s[:, :, 0, :, None] + top_s[:, :, 1, None, :]).reshape(C, PEER_HEADS, PEER_TOPK * PEER_TOPK)
        cand_i = (top_i[:, :, 0, :, None] * PEER_N_KEYS + top_i[:, :, 1, None, :]).reshape(C, PEER_HEADS, PEER_TOPK * PEER_TOPK)
        best_s, best_pos = lax.top_k(cand_s, PEER_TOPK)
        expert_idx = jnp.take_along_axis(cand_i, best_pos, axis=-1)
        g = jax.nn.softmax(best_s, axis=-1)
        u = expert_u[expert_idx]
        vv = expert_v[expert_idx]
        act = jax.nn.gelu(jnp.einsum('chkd,cd->chk', u, xc, preferred_element_type=jnp.float32))
        return jnp.einsum('chk,chkd->cd', (g * act).astype(vv.dtype), vv)

    out = lax.map(chunk, h.reshape(n_tok // C, C, D))
    return out.reshape(B, T, D)


def kernel(x, norm1_g, w_in, cmp_pe, cmp_k_w1, cmp_k_w2, cmp_v_w1, cmp_v_w2, q_norm_g, k_norm_g, rel_bias_table, sb_out_g, nsa_out_g, w_out, norm2_g, peer_w_query, peer_sub_keys, peer_expert_u, peer_expert_v):
    B, T, D = x.shape
    G, R, Dh = N_NSA_KV, NSA_GROUP, HEAD_DIM
    l = 0
    x2 = x.reshape(B * T, D)

    n_pad = (-IN_COLS) % 640
    w_in_p = jnp.pad(w_in[l], ((0, 0), (0, n_pad))).astype(jnp.bfloat16)
    proj = norm_matmul(x2, norm1_g[l], w_in_p).reshape(B, T, IN_COLS + n_pad)

    o_sb = stick_breaking_attention_pallas(proj, N_SB_HEADS, 0, N_SB_HEADS, 2 * N_SB_HEADS)

    cuts = [int(c) - 3 * SB_WIDTH for c in np.cumsum(IN_SPLITS)[3:-1]]
    q_nsa, k_cmp, v_cmp, k_slc, v_slc, k_win, v_win, gate_logits = jnp.split(
        proj[:, :, 3 * SB_WIDTH:IN_COLS].astype(jnp.float32), cuts, axis=-1)

    kv_heads = lambda t: t.reshape(B, T, G, Dh)
    to_bgtd = lambda t: t.transpose(0, 2, 1, 3)
    q = rms_norm(q_nsa.reshape(B, T, G, R, Dh), q_norm_g[l]).transpose(0, 2, 3, 1, 4)
    kc = to_bgtd(rms_norm(compress_blocks(kv_heads(k_cmp), cmp_pe[l], cmp_k_w1[l], cmp_k_w2[l]), k_norm_g[l]))
    vc = to_bgtd(compress_blocks(kv_heads(v_cmp), cmp_pe[l], cmp_v_w1[l], cmp_v_w2[l]))
    bf16 = jnp.bfloat16
    qs = (q * (Dh ** -0.5)).astype(bf16)
    n_cmp = kc.shape[2]
    ncp = -(-n_cmp // 128) * 128
    pad_c = lambda t: jnp.pad(t, ((0, 0), (0, 0), (0, ncp - n_cmp), (0, 0))).astype(bf16)
    near_bias, far_bias, tile_bias = nsa_bias_tables(rel_bias_table, T, ncp)
    o_cmp, sel_t = compressed_attention_select(qs, pad_c(kc), pad_c(vc), near_bias, far_bias, n_cmp)
    sel = jnp.swapaxes(sel_t, 2, 3).astype(bf16)
    o_slc = band_attention(qs, to_bgtd(rms_norm(kv_heads(k_slc), k_norm_g[l])).astype(bf16),
                           to_bgtd(kv_heads(v_slc)).astype(bf16), tile_bias, sel)
    o_win = band_attention(qs, to_bgtd(rms_norm(kv_heads(k_win), k_norm_g[l])).astype(bf16),
                           to_bgtd(kv_heads(v_win)).astype(bf16), tile_bias)
    gates = jax.nn.sigmoid(gate_logits).reshape(B, T, G, R, 3).transpose(0, 2, 3, 1, 4)
    o_nsa = gates[..., 0:1] * o_cmp + gates[..., 1:2] * o_slc + gates[..., 2:3] * o_win
    o_nsa = o_nsa.transpose(0, 3, 1, 2, 4).reshape(B, T, NSA_WIDTH)

    mixed = jnp.concatenate([rms_norm(o_sb, sb_out_g[l]), rms_norm(o_nsa, nsa_out_g[l])], axis=-1)
    x1 = matmul_residual(mixed.reshape(B * T, MIX_WIDTH), w_out[l].astype(jnp.bfloat16), x2)

    h2, idx_t, gate_t = peer_route(x1, norm2_g[l], peer_w_query[l], peer_sub_keys[l])
    uv = pack_expert_tables(peer_expert_u[l], peer_expert_v[l])
    out = peer_experts(x1, h2.T, idx_t.T, gate_t.T, uv)
    return out.reshape(B, T, D)
```

```python
import functools
import math

import numpy as np
import jax
import jax.numpy as jnp
from jax import lax
from jax.experimental import pallas as pl
from jax.experimental.pallas import tpu as pltpu

D_MODEL = 2048
HEAD_DIM = 128
N_SB_HEADS = 8
N_NSA_HEADS = 8
N_NSA_KV = 2
NSA_GROUP = N_NSA_HEADS // N_NSA_KV
SB_WIDTH = N_SB_HEADS * HEAD_DIM
NSA_WIDTH = N_NSA_HEADS * HEAD_DIM
KV_WIDTH = N_NSA_KV * HEAD_DIM
MIX_WIDTH = SB_WIDTH + NSA_WIDTH
N_GATES = 3 * N_NSA_HEADS
IN_SPLITS = (SB_WIDTH, SB_WIDTH, SB_WIDTH, NSA_WIDTH, KV_WIDTH, KV_WIDTH, KV_WIDTH, KV_WIDTH, KV_WIDTH, KV_WIDTH, N_GATES)
IN_COLS = sum(IN_SPLITS)
Q_BLOCK = 128
CMP_BLOCK = 32
CMP_STRIDE = 16
CMP_HIDDEN = 256
SEL_BLOCK = 64
N_SELECT = 16
SEL_Q_CHUNK = 32
WINDOW = 512
N_BUCKETS = 32
MAX_DISTANCE = 128
PEER_HEADS = 8
PEER_N_KEYS = 128
PEER_D_KEY = 256
PEER_TOPK = 16
PEER_TOKEN_CHUNK = 128
EPS = 1e-6
NEG = -1e30
BIG = 1e30

VMEM_LIMIT_BYTES = 56 * 1024 * 1024


def _norm_matmul_kernel(x_ref, g_ref, w_ref, hg_ref, hf_ref, o_ref, h_ref, *, first_head_norm_tile):
    j = pl.program_id(1)

    @pl.when(j == 0)
    def _():
        x = x_ref[...]
        ms = jnp.mean(x * x, axis=-1, keepdims=True)
        h_ref[...] = (x * lax.rsqrt(ms + EPS) * g_ref[...]).astype(jnp.bfloat16)

    @pl.when(j < first_head_norm_tile)
    def _():
        o_ref[...] = jnp.dot(h_ref[...], w_ref[...], preferred_element_type=jnp.float32).astype(o_ref.dtype)

    @pl.when(j >= first_head_norm_tile)
    def _():
        acc = jnp.dot(h_ref[...], w_ref[...], preferred_element_type=jnp.float32)
        for hb in range(o_ref.shape[1] // HEAD_DIM):
            cols = slice(hb * HEAD_DIM, (hb + 1) * HEAD_DIM)
            y = acc[:, cols]
            ms = jnp.mean(y * y, axis=-1, keepdims=True)
            yn = y * lax.rsqrt(ms + EPS) * hg_ref[:, cols]
            o_ref[:, cols] = jnp.where(hf_ref[:, cols] > 0.5, yn, y).astype(o_ref.dtype)


def norm_matmul(x, g, w, head_gain, head_flag, tm=1024, tn=640, out_dtype=jnp.bfloat16):
    m, k = x.shape
    n = w.shape[1]
    tm = min(tm, m)
    assert m % tm == 0 and n % tn == 0 and tn % HEAD_DIM == 0
    first_flagged = int(np.argmax(np.asarray(head_flag) > 0.5)) if np.any(np.asarray(head_flag) > 0.5) else n
    return pl.pallas_call(
        functools.partial(_norm_matmul_kernel, first_head_norm_tile=first_flagged // tn),
        grid=(m // tm, n // tn),
        in_specs=[
            pl.BlockSpec((tm, k), lambda i, j: (i, 0)),
            pl.BlockSpec((1, k), lambda i, j: (0, 0)),
            pl.BlockSpec((k, tn), lambda i, j: (0, j)),
            pl.BlockSpec((1, tn), lambda i, j: (0, j)),
            pl.BlockSpec((1, tn), lambda i, j: (0, j)),
        ],
        out_specs=pl.BlockSpec((tm, tn), lambda i, j: (i, j)),
        out_shape=jax.ShapeDtypeStruct((m, n), out_dtype),
        scratch_shapes=[pltpu.VMEM((tm, k), jnp.bfloat16)],
        compiler_params=pltpu.CompilerParams(
            dimension_semantics=("parallel", "arbitrary"),
            vmem_limit_bytes=VMEM_LIMIT_BYTES),
        name="norm_matmul",
    )(x, g.reshape(1, k), w, head_gain.reshape(1, n), jnp.asarray(head_flag, jnp.float32).reshape(1, n))


def _matmul_residual_kernel(a_ref, w_ref, r_ref, o_ref):
    o_ref[...] = r_ref[...] + jnp.dot(a_ref[...].astype(jnp.bfloat16), w_ref[...],
                                      preferred_element_type=jnp.float32)


def matmul_residual(a, w, r, tm=512, tn=512):
    m, k = a.shape
    n = w.shape[1]
    assert m % tm == 0 and n % tn == 0
    return pl.pallas_call(
        _matmul_residual_kernel,
        grid=(m // tm, n // tn),
        in_specs=[
            pl.BlockSpec((tm, k), lambda i, j: (i, 0)),
            pl.BlockSpec((k, tn), lambda i, j: (0, j)),
            pl.BlockSpec((tm, tn), lambda i, j: (i, j)),
        ],
        out_specs=pl.BlockSpec((tm, tn), lambda i, j: (i, j)),
        out_shape=jax.ShapeDtypeStruct((m, n), jnp.float32),
        compiler_params=pltpu.CompilerParams(
            dimension_semantics=("parallel", "arbitrary"),
            vmem_limit_bytes=VMEM_LIMIT_BYTES),
        name="matmul_residual",
    )(a, w, r)


SB_Q_TILE = 256
SB_K_TILE = 128
F32_EXP_UNDERFLOW = -104.0


def _sb_attn_kernel(q_ref, k_ref, v_ref, o_ref, c_ref, acc_ref):
    tq, dh = q_ref.shape
    tk = SB_K_TILE
    qi = pl.program_id(2)
    q = q_ref[...]
    scale = dh ** -0.5
    c_ref[...] = jnp.zeros(c_ref.shape, jnp.float32)
    acc_ref[...] = jnp.zeros(acc_ref.shape, jnp.float32)
    jj = lax.broadcasted_iota(jnp.int32, (2 * tk, 2 * tk), 0) % tk
    ss = lax.broadcasted_iota(jnp.int32, (2 * tk, 2 * tk), 1)
    suffix_and_total = jnp.where((jj > ss) | (ss >= tk), 1.0, 0.0).astype(jnp.bfloat16)
    row = lax.broadcasted_iota(jnp.int32, (tq, tk), 0)
    col = lax.broadcasted_iota(jnp.int32, (tq, tk), 1)

    def tile_terms(kj, masked, may_be_absent):
        live = (kj >= 0).astype(jnp.float32) if may_be_absent else 1.0
        start = pl.multiple_of(jnp.maximum(kj, 0) * tk, tk)
        k = k_ref[pl.ds(start, tk), :]
        v = v_ref[pl.ds(start, tk), :]
        z = lax.dot_general(q, k, (((1,), (1,)), ((), ())), preferred_element_type=jnp.float32) * scale
        t = jnp.log(1.0 + jnp.exp(-jnp.abs(z)))
        log_keep = -(jnp.maximum(z, 0.0) + t) * live
        log_beta = jnp.minimum(z, 0.0) - t
        keep = live
        if masked:
            before = (kj * tk + col) < (qi * tq + row)
            log_keep = jnp.where(before, log_keep, 0.0)
            keep = jnp.where(before, live, 0.0)
        hi = log_keep.astype(jnp.bfloat16)
        lo = (log_keep - hi.astype(jnp.float32)).astype(jnp.bfloat16)
        sums = jnp.dot(jnp.concatenate([hi, lo], axis=1), suffix_and_total, preferred_element_type=jnp.float32)
        return log_beta + sums[:, :tk], sums[:, tk:], keep, v

    def tile_pair(kj, masked):
        lw_a, tot_a, keep_a, v_a = tile_terms(kj, masked, False)
        lw_b, tot_b, keep_b, v_b = tile_terms(kj - 1, masked, not masked)
        c = c_ref[...]
        a_a = jnp.exp(lw_a + c)
        if masked:
            a_a = a_a * keep_a
        a_b = jnp.exp(lw_b + (c + tot_a)) * keep_b
        a = jnp.concatenate([a_a, a_b], axis=1).astype(jnp.bfloat16)
        acc_ref[...] += jnp.dot(a, jnp.concatenate([v_a, v_b], axis=0), preferred_element_type=jnp.float32)
        c_new = c + tot_a + tot_b
        c_ref[...] = c_new
        return jnp.max(c_new)

    n_diag = tq // tk
    assert n_diag == 2
    c_max = tile_pair(qi * n_diag + 1, True)

    def cond(carry):
        kj, c_max = carry
        return (kj >= 0) & (c_max > F32_EXP_UNDERFLOW)

    def body(carry):
        kj, _ = carry
        return kj - 2, tile_pair(kj, False)

    lax.while_loop(cond, body, (qi * n_diag - 1, c_max))
    o_ref[...] = acc_ref[...]


def stick_breaking_attention_pallas(proj, n_heads, q_col, k_col, v_col):
    B, T, _ = proj.shape
    dh = HEAD_DIM
    tq = min(SB_Q_TILE, T)
    return pl.pallas_call(
        _sb_attn_kernel,
        grid=(B, n_heads, T // tq),
        in_specs=[
            pl.BlockSpec((None, tq, dh), lambda b, h, i: (b, i, q_col + h)),
            pl.BlockSpec((None, T, dh), lambda b, h, i: (b, 0, k_col + h)),
            pl.BlockSpec((None, T, dh), lambda b, h, i: (b, 0, v_col + h)),
        ],
        out_specs=pl.BlockSpec((None, tq, dh), lambda b, h, i: (b, i, h)),
        out_shape=jax.ShapeDtypeStruct((B, T, n_heads * dh), jnp.float32),
        scratch_shapes=[
            pltpu.VMEM((tq, SB_K_TILE), jnp.float32),
            pltpu.VMEM((tq, dh), jnp.float32),
        ],
        compiler_params=pltpu.CompilerParams(
            dimension_semantics=("parallel", "parallel", "arbitrary"),
            vmem_limit_bytes=VMEM_LIMIT_BYTES),
        name="sb_attention",
    )(proj, proj, proj)


def _bias_saturation_distance():
    max_exact = N_BUCKETS // 2
    return int(math.ceil(max_exact * (MAX_DISTANCE / max_exact) ** ((N_BUCKETS - 1 - max_exact) / (N_BUCKETS - max_exact))))


def _cmp_attn_kernel(q_ref, kc_ref, vc_ref, near_ref, far_ref, ov_ref, o_ref, selt_ref, *, n_cmp, n_sel):
    qb = q_ref.shape[0]
    ncp, dh = kc_ref.shape
    r = q_ref.shape[1] // dh
    nb = ov_ref.shape[0]
    qi = pl.program_id(2)
    q = _stack_heads(q_ref[...], r)
    s = lax.dot_general(q, kc_ref[...], (((1,), (1,)), ((), ())), preferred_element_type=jnp.float32)
    wn = near_ref.shape[-1]
    first = qi * (qb // CMP_STRIDE) - qb // CMP_STRIDE
    place = jnp.where(lax.broadcasted_iota(jnp.int32, (wn, ncp), 1)
                      == first + lax.broadcasted_iota(jnp.int32, (wn, ncp), 0), 1.0, 0.0).astype(jnp.bfloat16)
    near = near_ref[...].reshape(r * qb, wn)
    near_hi = near.astype(jnp.bfloat16)
    near_lo = (near - near_hi.astype(jnp.float32)).astype(jnp.bfloat16)
    s = s + (jnp.dot(near_hi, place, preferred_element_type=jnp.float32)
             + jnp.dot(near_lo, place, preferred_element_type=jnp.float32))
    s = s.reshape(r, qb, ncp) + far_ref[...]
    tpos = qi * qb + lax.broadcasted_iota(jnp.int32, (qb, ncp), 0)
    c = lax.broadcasted_iota(jnp.int32, (qb, ncp), 1)
    valid = ((tpos - (c * CMP_STRIDE + CMP_BLOCK - 1)) >= 0) & (c < n_cmp)
    s = jnp.where(valid[None], s, NEG)
    m = jnp.max(s, axis=-1, keepdims=True)
    e = jnp.where(valid[None], jnp.exp(s - m), 0.0)
    l = jnp.sum(e, axis=-1, keepdims=True)
    p = e * jnp.where(l > 0.0, 1.0 / l, 0.0)
    o = jnp.dot(p.reshape(r * qb, ncp).astype(jnp.bfloat16), vc_ref[...], preferred_element_type=jnp.float32)
    o_ref[...] = _unstack_heads(o, r)

    psum = jnp.sum(p, axis=0)
    p_hi = psum.astype(jnp.bfloat16)
    p_lo = (psum - p_hi.astype(jnp.float32)).astype(jnp.bfloat16)
    nt = (((1,), (1,)), ((), ()))
    imp_t = (lax.dot_general(ov_ref[...], p_hi, nt, preferred_element_type=jnp.float32)
             + lax.dot_general(ov_ref[...], p_lo, nt, preferred_element_type=jnp.float32))

    blk = lax.broadcasted_iota(jnp.int32, (nb, qb), 0)
    cur = (qi * qb + lax.broadcasted_iota(jnp.int32, (nb, qb), 1)) // SEL_BLOCK
    forced = (blk == 0) | (blk == cur) | (blk == cur - 1)
    score = jnp.where(forced, BIG, jnp.where(blk > cur, NEG, imp_t))
    rank = jnp.zeros((nb, qb), jnp.float32)
    for mblk in range(nb):
        row = score[mblk:mblk + 1, :]
        ahead = (row > score) | ((row == score) & (blk > mblk))
        rank = rank + jnp.where(ahead, 1.0, 0.0)
    selt_ref[...] = jnp.where(rank < float(n_sel), 1.0, 0.0)


def compressed_attention_select(proj, q_col, kc, vc, near_bias, far_bias, n_cmp):
    B, T, _ = proj.shape
    G, R = near_bias.shape[:2]
    Dh = HEAD_DIM
    assert q_col % R == 0
    ncp = kc.shape[2]
    nb = T // SEL_BLOCK
    n_sel = min(N_SELECT, nb)
    c0 = np.arange(ncp)[None, :] * CMP_STRIDE
    s0 = np.arange(nb)[:, None] * SEL_BLOCK
    ov = np.clip(np.minimum(c0 + CMP_BLOCK, s0 + SEL_BLOCK) - np.maximum(c0, s0), 0, None).astype(np.float32) / np.float32(CMP_BLOCK)
    ov = np.where(np.arange(ncp)[None, :] < n_cmp, ov, 0.0)
    ov_t = jnp.asarray(ov, dtype=jnp.bfloat16)
    qb = Q_BLOCK
    return pl.pallas_call(
        functools.partial(_cmp_attn_kernel, n_cmp=n_cmp, n_sel=n_sel),
        grid=(B, G, T // qb),
        in_specs=[
            pl.BlockSpec((None, qb, R * Dh), lambda b, g, i: (b, i, q_col // R + g)),
            pl.BlockSpec((None, None, ncp, Dh), lambda b, g, i: (b, g, 0, 0)),
            pl.BlockSpec((None, None, ncp, Dh), lambda b, g, i: (b, g, 0, 0)),
            pl.BlockSpec((None, R, qb, LANES), lambda b, g, i: (g, 0, 0, 0)),
            pl.BlockSpec((None, R, 1, ncp), lambda b, g, i: (g, 0, 0, 0)),
            pl.BlockSpec((nb, ncp), lambda b, g, i: (0, 0)),
        ],
        out_specs=[
            pl.BlockSpec((None, qb, R * Dh), lambda b, g, i: (b, i, g)),
            pl.BlockSpec((None, None, nb, qb), lambda b, g, i: (b, g, 0, i)),
        ],
        out_shape=[
            jax.ShapeDtypeStruct((B, T, G * R * Dh), jnp.float32),
            jax.ShapeDtypeStruct((B, G, nb, T), jnp.float32),
        ],
        compiler_params=pltpu.CompilerParams(
            dimension_semantics=("parallel", "parallel", "arbitrary"),
            vmem_limit_bytes=VMEM_LIMIT_BYTES),
        name="cmp_attention_select",
    )(proj, kc, vc, near_bias, far_bias, ov_t)


BAND_SLC_TILES_PER_STEP = 4


def _band_attn_kernel(*refs, mode):
    if mode == "slc":
        q_ref, k_ref, v_ref, bias_ref, sel_ref, o_ref, m_ref, l_ref, acc_ref = refs
    else:
        q_ref, k_ref, v_ref, bias_ref, o_ref, m_ref, l_ref, acc_ref = refs
    qb = q_ref.shape[0]
    dh = k_ref.shape[1]
    r = q_ref.shape[1] // dh
    qi = pl.program_id(2)
    q = _stack_heads(q_ref[...], r)
    m_ref[...] = jnp.full(m_ref.shape, -jnp.inf, jnp.float32)
    l_ref[...] = jnp.zeros(l_ref.shape, jnp.float32)
    acc_ref[...] = jnp.zeros(acc_ref.shape, jnp.float32)
    row = lax.broadcasted_iota(jnp.int32, (qb, qb), 0)
    col = lax.broadcasted_iota(jnp.int32, (qb, qb), 1)
    if mode == "slc":
        nb = sel_ref.shape[1]
        sel = sel_ref[...]
        blk_of_col = lax.broadcasted_iota(jnp.int32, (nb, qb), 1) // SEL_BLOCK
        blk_row = lax.broadcasted_iota(jnp.int32, (nb, qb), 0)
        n_tiles = qi + 1
        kt = BAND_SLC_TILES_PER_STEP
    else:
        n_tiles = jnp.minimum(qi, WINDOW // qb) + 1
        kt = WINDOW // qb + 1
    n_steps = (n_tiles + kt - 1) // kt

    def body(step, carry):
        s_parts, v_parts = [], []
        for u in range(kt):
            off = step * kt + u
            kj = qi - off
            dead = jnp.where(off < n_tiles, 0, 1 << 20)
            start = pl.multiple_of(jnp.maximum(kj, 0) * qb, qb)
            k = k_ref[pl.ds(start, qb), :]
            s = lax.dot_general(q, k, (((1,), (1,)), ((), ())), preferred_element_type=jnp.float32)
            dist = off * qb + row - col
            if mode == "slc":
                expand = jnp.where(blk_row == blk_of_col + kj * (qb // SEL_BLOCK), 1.0, 0.0).astype(jnp.bfloat16)
                picked = jnp.dot(sel, expand, preferred_element_type=jnp.float32)
                valid = (picked > 0.5) & (dist - dead >= 0)
            else:
                valid = (dist >= 0) & (dist + dead < WINDOW)
            addmask = jnp.where(valid, 0.0, NEG)
            s = s.reshape(r, qb, qb) + (bias_ref[jnp.minimum(off, 2)] + addmask[None])
            s_parts.append(s.reshape(r * qb, qb))
            v_parts.append(v_ref[pl.ds(start, qb), :])
        s = jnp.concatenate(s_parts, axis=1)
        v = jnp.concatenate(v_parts, axis=0)
        m_prev = m_ref[...]
        m_new = jnp.maximum(m_prev, jnp.max(s, axis=-1, keepdims=True))
        alpha = jnp.exp(m_prev - m_new)
        p = jnp.exp(s - m_new)
        l_ref[...] = alpha * l_ref[...] + jnp.sum(p, axis=-1, keepdims=True)
        acc_ref[...] = alpha * acc_ref[...] + jnp.dot(p.astype(jnp.bfloat16), v, preferred_element_type=jnp.float32)
        m_ref[...] = m_new
        return carry

    lax.fori_loop(0, n_steps, body, 0)
    o_ref[...] = _unstack_heads(acc_ref[...] / l_ref[...], r)


def _stack_heads(x, r):
    dh = x.shape[1] // r
    return jnp.concatenate([x[:, h * dh:(h + 1) * dh] for h in range(r)], axis=0)


def _unstack_heads(x, r):
    qb = x.shape[0] // r
    return jnp.concatenate([x[h * qb:(h + 1) * qb, :] for h in range(r)], axis=1)


def band_attention(proj, q_col, k_col, v_col, tile_bias, sel=None):
    B, T, _ = proj.shape
    _, G, R, qb, _ = tile_bias.shape
    Dh = HEAD_DIM
    assert q_col % R == 0
    mode = "win" if sel is None else "slc"
    in_specs = [
        pl.BlockSpec((None, qb, R * Dh), lambda b, g, i: (b, i, q_col // R + g)),
        pl.BlockSpec((None, T, Dh), lambda b, g, i: (b, 0, k_col + g)),
        pl.BlockSpec((None, T, Dh), lambda b, g, i: (b, 0, v_col + g)),
        pl.BlockSpec((3, None, R, qb, qb), lambda b, g, i: (0, g, 0, 0, 0)),
    ]
    args = [proj, proj, proj, tile_bias]
    if sel is not None:
        nb = sel.shape[-1]
        in_specs.append(pl.BlockSpec((None, None, qb, nb), lambda b, g, i: (b, g, i, 0)))
        args.append(sel)
    return pl.pallas_call(
        functools.partial(_band_attn_kernel, mode=mode),
        grid=(B, G, T // qb),
        in_specs=in_specs,
        out_specs=pl.BlockSpec((None, qb, R * Dh), lambda b, g, i: (b, i, g)),
        out_shape=jax.ShapeDtypeStruct((B, T, G * R * Dh), jnp.float32),
        scratch_shapes=[
            pltpu.VMEM((R * qb, 1), jnp.float32),
            pltpu.VMEM((R * qb, 1), jnp.float32),
            pltpu.VMEM((R * qb, Dh), jnp.float32),
        ],
        compiler_params=pltpu.CompilerParams(
            dimension_semantics=("parallel", "parallel", "arbitrary"),
            vmem_limit_bytes=VMEM_LIMIT_BYTES),
        name="band_attention_" + mode,
    )(*args)


def nsa_bias_tables(table, T, ncp):
    sat = _bias_saturation_distance()
    assert sat <= Q_BLOCK + 1
    G, R = N_NSA_KV, NSA_GROUP
    back = Q_BLOCK // CMP_STRIDE
    width = 2 * back
    assert (back + 1) * CMP_STRIDE - (CMP_BLOCK - 1) >= sat and width <= LANES
    assert (Q_BLOCK - 1) - back * CMP_STRIDE - (CMP_BLOCK - 1) < 0
    tl = jnp.arange(Q_BLOCK)
    far_row = table[N_BUCKETS - 1]
    dist_n = tl[:, None] - (jnp.arange(width)[None, :] - back) * CMP_STRIDE - (CMP_BLOCK - 1)
    near = jnp.take(table, t5_bucket(dist_n), axis=0) - far_row
    near = jnp.pad(jnp.moveaxis(near, -1, 0), ((0, 0), (0, 0), (0, LANES - width))).reshape(G, R, Q_BLOCK, LANES)
    far = jnp.broadcast_to(far_row.reshape(G, R, 1, 1), (G, R, 1, ncp))
    dist_t = jnp.arange(3)[:, None, None] * Q_BLOCK + tl[None, :, None] - tl[None, None, :]
    tile_bias = jnp.moveaxis(jnp.take(table, t5_bucket(dist_t), axis=0), -1, 1).reshape(3, G, R, Q_BLOCK, Q_BLOCK)
    return near, far, tile_bias


PEER_ROUTE_TM = 256
PEER_HALF = PEER_D_KEY // 2


def _top_rows(s, order, count, payload=None):
    vals, picks = [], []
    for _ in range(count):
        m = jnp.max(s, axis=0, keepdims=True)
        o = jnp.min(jnp.where(s == m, order, jnp.inf), axis=0, keepdims=True)
        hit = order == o
        vals.append(m)
        picks.append(o if payload is None else jnp.max(jnp.where(hit, payload, -1.0), axis=0, keepdims=True))
        s = jnp.where(hit, -jnp.inf, s)
    return vals, picks


def _peer_route_kernel(x_ref, g_ref, wq_ref, keys_ref, h_ref, idx_ref, gate_ref):
    tm = x_ref.shape[0]
    nk = keys_ref.shape[1]
    topk = PEER_TOPK
    x = x_ref[...]
    ms = jnp.mean(x * x, axis=-1, keepdims=True)
    h = (x * lax.rsqrt(ms + EPS) * g_ref[...]).astype(jnp.bfloat16)
    h_ref[...] = h
    key_iota = lax.broadcasted_iota(jnp.int32, (nk, tm), 0).astype(jnp.float32)

    def head_body(head, carry):
        tops = []
        for p in range(2):
            hp = head * 2 + p
            qhp = jnp.dot(h, wq_ref[hp], preferred_element_type=jnp.float32).astype(jnp.bfloat16)
            s = lax.dot_general(keys_ref[hp], qhp, (((1,), (1,)), ((), ())), preferred_element_type=jnp.float32)
            vals, picks = _top_rows(s, key_iota, topk)
            tops.append((jnp.concatenate(vals, axis=0), jnp.concatenate(picks, axis=0)))
        (s0, i0), (s1, i1) = tops
        cs, cpos, ce = [], [], []

        def add(rows_s, rows_pos, rows_e):
            cs.append(rows_s); cpos.append(rows_pos); ce.append(rows_e)

        jr16 = lax.broadcasted_iota(jnp.int32, (topk, tm), 0).astype(jnp.float32)
        add(s0[0:1] + s1, jr16, i0[0:1] * nk + i1)
        jr8 = lax.broadcasted_iota(jnp.int32, (8, tm), 0).astype(jnp.float32)
        for i in range(1, 8):
            nj = topk // (i + 1)
            vals = s0[i:i + 1] + s1[0:8]
            add(jnp.where(jr8 < nj, vals, -jnp.inf), i * topk + jr8, i0[i:i + 1] * nk + i1[0:8])
        add(s0[8:16] + s1[0:1], (8 + jr8) * topk, i0[8:16] * nk + i1[0:1])
        cand_s = jnp.concatenate(cs, axis=0)
        cand_pos = jnp.concatenate(cpos, axis=0)
        cand_e = jnp.concatenate(ce, axis=0)
        best_s, best_e = _top_rows(cand_s, cand_pos, topk, payload=cand_e)
        bs = jnp.concatenate(best_s, axis=0)
        ex = jnp.exp(bs - bs[0:1])
        gate = ex / jnp.sum(ex, axis=0, keepdims=True)
        row0 = pl.multiple_of(head * topk, topk)
        idx_ref[pl.ds(row0, topk), :] = jnp.concatenate(best_e, axis=0).astype(jnp.int32)
        gate_ref[pl.ds(row0, topk), :] = gate
        return carry

    lax.fori_loop(0, PEER_HEADS, head_body, 0)


def peer_route(x1, g, w_query, sub_keys):
    n, d = x1.shape
    assert PEER_TOPK == 16 and PEER_N_KEYS % 8 == 0
    tm = min(PEER_ROUTE_TM, n)
    hk = PEER_HEADS * PEER_TOPK
    wq = w_query.reshape(d, 2 * PEER_HEADS, PEER_HALF).transpose(1, 0, 2).astype(jnp.bfloat16)
    keys = sub_keys.reshape(2 * PEER_HEADS, PEER_N_KEYS, PEER_HALF).astype(jnp.bfloat16)
    return pl.pallas_call(
        _peer_route_kernel,
        grid=(n // tm,),
        in_specs=[
            pl.BlockSpec((tm, d), lambda i: (i, 0)),
            pl.BlockSpec((1, d), lambda i: (0, 0)),
            pl.BlockSpec((2 * PEER_HEADS, d, PEER_HALF), lambda i: (0, 0, 0)),
            pl.BlockSpec((2 * PEER_HEADS, PEER_N_KEYS, PEER_HALF), lambda i: (0, 0, 0)),
        ],
        out_specs=[
            pl.BlockSpec((tm, d), lambda i: (i, 0)),
            pl.BlockSpec((hk, tm), lambda i: (0, i)),
            pl.BlockSpec((hk, tm), lambda i: (0, i)),
        ],
        out_shape=[
            jax.ShapeDtypeStruct((n, d), jnp.bfloat16),
            jax.ShapeDtypeStruct((hk, n), jnp.int32),
            jax.ShapeDtypeStruct((hk, n), jnp.float32),
        ],
        compiler_params=pltpu.CompilerParams(
            dimension_semantics=("parallel",),
            vmem_limit_bytes=VMEM_LIMIT_BYTES),
        name="peer_route",
    )(x1, g.reshape(1, d), wq, keys)


PEER_TB = 8
PEER_SLOTS = 3
PEER_USES = PEER_HEADS * PEER_TOPK
PEER_ROWS = PEER_TB * PEER_USES
LANES = 128


def _peer_expert_kernel(idx0_ref, idx1_ref, idx2_ref, ht_ref, gate_ref, x_ref, uv_hbm, o_ref, buf, sem, act_t):
    i = pl.program_id(0)
    n = pl.num_programs(0)
    tb, d = x_ref.shape
    ns = d // LANES
    uses = PEER_USES

    def row_copy(idx_ref, slot, tok, j):
        src = uv_hbm.at[pl.ds(pl.multiple_of(idx_ref[tok * uses + j] * ns, ns), ns)]
        return pltpu.make_async_copy(src, buf.at[slot, tok, :, j, :], sem.at[slot])

    def slot_wait(slot):
        pltpu.make_async_copy(buf.at[slot], buf.at[slot], sem.at[slot]).wait()

    @pl.when(i == 0)
    def _():
        for tok in range(tb):
            def first(j, c):
                row_copy(idx0_ref, 0, tok, j).start()
                row_copy(idx1_ref, 1, tok, j).start()
                return c
            lax.fori_loop(0, uses, first, 0)

    slot = i % PEER_SLOTS
    slot_next = (i + 2) % PEER_SLOTS
    slot_wait(slot)

    half = uses // 2
    col0 = (i % (LANES // tb)) * tb
    lane = lax.broadcasted_iota(jnp.int32, (uses, LANES), 1)

    acts = jnp.zeros((uses, LANES), jnp.float32)
    for tok in range(tb):
        for j in range(half):
            row_copy(idx2_ref, slot_next, tok, j).start()
        u = jnp.concatenate(
            [lax.bitcast_convert_type(buf[slot, tok, s] & jnp.uint32(0xFFFF0000), jnp.float32).astype(jnp.bfloat16)
             for s in range(ns)], axis=1)
        r = jnp.dot(u, ht_ref[...], preferred_element_type=jnp.float32)
        acts = jnp.where(lane == col0 + tok, r, acts)
    act_t[...] = acts.T
    act = act_t[pl.ds(pl.multiple_of(col0, tb), tb), :]
    coef = (gate_ref[...] * jax.nn.gelu(act)).astype(jnp.bfloat16)

    for tok in range(tb):
        for j in range(half, uses):
            row_copy(idx2_ref, slot_next, tok, j).start()
        for s in range(ns):
            v = lax.bitcast_convert_type(buf[slot, tok, s] << 16, jnp.float32).astype(jnp.bfloat16)
            y = jnp.dot(coef, v, preferred_element_type=jnp.float32)
            cols = slice(s * LANES, (s + 1) * LANES)
            o_ref[tok:tok + 1, cols] = x_ref[tok:tok + 1, cols] + y[tok:tok + 1, :]

    @pl.when(i == n - 1)
    def _():
        slot_wait((i + 1) % PEER_SLOTS)
        slot_wait((i + 2) % PEER_SLOTS)


def peer_experts(x1, h_t, idx, gate, uv):
    n, d = x1.shape
    tb = PEER_TB
    nblk = n // tb
    assert n % LANES == 0 and LANES % tb == 0 and d % LANES == 0 and PEER_USES == LANES
    idx_flat = idx.reshape(n * PEER_USES)
    last = nblk - 1
    smem_spec = lambda ahead: pl.BlockSpec((PEER_ROWS,), lambda i: (jnp.minimum(i + ahead, last),),
                                           memory_space=pltpu.SMEM)
    return pl.pallas_call(
        _peer_expert_kernel,
        grid=(nblk,),
        in_specs=[
            smem_spec(0), smem_spec(1), smem_spec(2),
            pl.BlockSpec((d, LANES), lambda i: (0, i // (LANES // tb))),
            pl.BlockSpec((tb, PEER_USES), lambda i: (i, 0)),
            pl.BlockSpec((tb, d), lambda i: (i, 0)),
            pl.BlockSpec(memory_space=pl.ANY),
        ],
        out_specs=pl.BlockSpec((tb, d), lambda i: (i, 0)),
        out_shape=jax.ShapeDtypeStruct((n, d), jnp.float32),
        scratch_shapes=[
            pltpu.VMEM((PEER_SLOTS, tb, d // LANES, PEER_USES, LANES), jnp.uint32),
            pltpu.SemaphoreType.DMA((PEER_SLOTS,)),
            pltpu.VMEM((LANES, PEER_USES), jnp.float32),
        ],
        compiler_params=pltpu.CompilerParams(
            dimension_semantics=("arbitrary",),
            vmem_limit_bytes=VMEM_LIMIT_BYTES),
        name="peer_experts",
    )(idx_flat, idx_flat, idx_flat, h_t, gate, x1, uv)


def pack_expert_tables(expert_u, expert_v):
    e, d = expert_u.shape
    hi = lax.bitcast_convert_type(expert_u.astype(jnp.bfloat16), jnp.uint16).astype(jnp.uint32)
    lo = lax.bitcast_convert_type(expert_v.astype(jnp.bfloat16), jnp.uint16).astype(jnp.uint32)
    return ((hi << 16) | lo).reshape(e * (d // LANES), LANES)


def rms_norm(x, g):
    xf = x.astype(jnp.float32)
    y = xf * lax.rsqrt(jnp.mean(xf * xf, axis=-1, keepdims=True) + EPS)
    return (y * g.astype(jnp.float32)).astype(x.dtype)


def masked_softmax(s, valid):
    p = jax.nn.softmax(jnp.where(valid, s, NEG), axis=-1)
    return jnp.where(valid, p, 0.0)


def t5_bucket(dist):
    max_exact = N_BUCKETS // 2
    d = jnp.maximum(dist, 0)
    log_ratio = jnp.log(jnp.maximum(d, 1).astype(jnp.float32) / max_exact) / math.log(MAX_DISTANCE / max_exact)
    large = jnp.minimum(max_exact + (log_ratio * (N_BUCKETS - max_exact)).astype(jnp.int32), N_BUCKETS - 1)
    return jnp.where(d < max_exact, d, large)


def rel_bias_grid(table, dist):
    b = jnp.take(table, t5_bucket(dist), axis=0)
    return jnp.moveaxis(b, -1, 0).reshape(N_NSA_KV, NSA_GROUP, *dist.shape)


def stick_breaking_attention(q, k, v):
    B, H, T, Dh = q.shape
    scale = Dh ** -0.5
    kpos = jnp.arange(T)

    def block(i):
        qb = lax.dynamic_slice_in_dim(q, i * Q_BLOCK, Q_BLOCK, axis=2)
        qpos = i * Q_BLOCK + jnp.arange(Q_BLOCK)
        z = jnp.einsum('bhqd,bhkd->bhqk', qb, k, preferred_element_type=jnp.float32) * scale
        before = kpos[None, :] < qpos[:, None]
        log_keep = jnp.where(before, -jax.nn.softplus(z), 0.0)
        log_between = lax.cumsum(log_keep, axis=3, reverse=True) - log_keep
        a = jnp.where(before, jnp.exp(jax.nn.log_sigmoid(z) + log_between), 0.0)
        return jnp.einsum('bhqk,bhkd->bqhd', a.astype(v.dtype), v)

    out = lax.map(block, jnp.arange(T // Q_BLOCK))
    return jnp.moveaxis(out, 0, 1).reshape(B, T, H * Dh)


def compress_blocks(kv, pe, w1, w2):
    B, T, G, Dh = kv.shape
    nc = (T - CMP_BLOCK) // CMP_STRIDE + 1
    idx = jnp.arange(nc)[:, None] * CMP_STRIDE + jnp.arange(CMP_BLOCK)[None, :]
    blocks = kv[:, idx] + pe[None, None, :, None, :]
    flat = jnp.moveaxis(blocks, 3, 2).reshape(B, nc, G, CMP_BLOCK * Dh)
    return jax.nn.gelu(flat @ w1) @ w2


def block_overlap(nc, nb):
    c0 = np.arange(nc)[:, None] * CMP_STRIDE
    s0 = np.arange(nb)[None, :] * SEL_BLOCK
    ov = np.minimum(c0 + CMP_BLOCK, s0 + SEL_BLOCK) - np.maximum(c0, s0)
    return jnp.asarray(np.clip(ov, 0, None).astype(np.float32) / np.float32(CMP_BLOCK))


def compressed_attention(q, kc, vc, table):
    B, G, R, T, Dh = q.shape
    nc = kc.shape[2]
    nb = T // SEL_BLOCK
    scale = Dh ** -0.5
    c_end = jnp.arange(nc) * CMP_STRIDE + CMP_BLOCK - 1
    overlap = block_overlap(nc, nb)

    def block(i):
        qb = lax.dynamic_slice_in_dim(q, i * Q_BLOCK, Q_BLOCK, axis=3)
        qpos = i * Q_BLOCK + jnp.arange(Q_BLOCK)
        dist = qpos[:, None] - c_end[None, :]
        s = jnp.einsum('bgrqd,bgcd->bgrqc', qb, kc, preferred_element_type=jnp.float32) * scale + rel_bias_grid(table, dist)
        p = masked_softmax(s, dist >= 0)
        o = jnp.einsum('bgrqc,bgcd->bgrqd', p.astype(vc.dtype), vc)
        imp = jnp.einsum('bgrqc,cn->bgqn', p, overlap)
        return o, imp

    o, imp = lax.map(block, jnp.arange(T // Q_BLOCK))
    return (jnp.moveaxis(o, 0, 3).reshape(B, G, R, T, Dh), jnp.moveaxis(imp, 0, 2).reshape(B, G, T, nb))


def selected_attention(q, k, v, imp, table):
    B, G, R, T, Dh = q.shape
    nb = T // SEL_BLOCK
    n_sel = min(N_SELECT, nb)
    scale = Dh ** -0.5
    blk = jnp.arange(nb)
    cur = jnp.arange(T) // SEL_BLOCK
    forced = (blk[None, :] == 0) | (blk[None, :] == cur[:, None]) | (blk[None, :] == cur[:, None] - 1)
    future = blk[None, :] > cur[:, None]
    score = jnp.where(forced, BIG, jnp.where(future, NEG, imp))
    _, sel = lax.top_k(score, n_sel)
    kb = k.reshape(B, G, nb, SEL_BLOCK, Dh)
    vb = v.reshape(B, G, nb, SEL_BLOCK, Dh)
    gather = jax.vmap(jax.vmap(lambda blocks, ix: blocks[ix]))
    tab_gr = jnp.swapaxes(table.reshape(N_BUCKETS, G, R), 0, 1)
    g_index = jnp.arange(G)[None, :, None, None]
    n_keys = n_sel * SEL_BLOCK

    def chunk(i):
        qc = lax.dynamic_slice_in_dim(q, i * SEL_Q_CHUNK, SEL_Q_CHUNK, axis=3)
        ic = lax.dynamic_slice_in_dim(sel, i * SEL_Q_CHUNK, SEL_Q_CHUNK, axis=2)
        qpos = i * SEL_Q_CHUNK + jnp.arange(SEL_Q_CHUNK)
        kg = gather(kb, ic).reshape(B, G, SEL_Q_CHUNK, n_keys, Dh)
        vg = gather(vb, ic).reshape(B, G, SEL_Q_CHUNK, n_keys, Dh)
        kpos = (ic[..., None] * SEL_BLOCK + jnp.arange(SEL_BLOCK)).reshape(B, G, SEL_Q_CHUNK, n_keys)
        dist = qpos[None, None, :, None] - kpos
        bias = jnp.moveaxis(tab_gr[g_index, t5_bucket(dist)], -1, 2)
        s = jnp.einsum('bgrqd,bgqkd->bgrqk', qc, kg, preferred_element_type=jnp.float32) * scale + bias
        p = masked_softmax(s, (dist >= 0)[:, :, None])
        return jnp.einsum('bgrqk,bgqkd->bgrqd', p.astype(vg.dtype), vg)

    o = lax.map(chunk, jnp.arange(T // SEL_Q_CHUNK))
    return jnp.moveaxis(o, 0, 3).reshape(B, G, R, T, Dh)


def window_attention(q, k, v, table):
    B, G, R, T, Dh = q.shape
    scale = Dh ** -0.5
    span = WINDOW + Q_BLOCK
    kp = jnp.pad(k, ((0, 0), (0, 0), (WINDOW, 0), (0, 0)))
    vp = jnp.pad(v, ((0, 0), (0, 0), (WINDOW, 0), (0, 0)))

    def block(i):
        qb = lax.dynamic_slice_in_dim(q, i * Q_BLOCK, Q_BLOCK, axis=3)
        kb = lax.dynamic_slice_in_dim(kp, i * Q_BLOCK, span, axis=2)
        vb = lax.dynamic_slice_in_dim(vp, i * Q_BLOCK, span, axis=2)
        qpos = i * Q_BLOCK + jnp.arange(Q_BLOCK)
        kpos = i * Q_BLOCK - WINDOW + jnp.arange(span)
        dist = qpos[:, None] - kpos[None, :]
        valid = (dist >= 0) & (dist < WINDOW) & (kpos[None, :] >= 0)
        s = jnp.einsum('bgrqd,bgkd->bgrqk', qb, kb, preferred_element_type=jnp.float32) * scale + rel_bias_grid(table, dist)
        p = masked_softmax(s, valid)
        return jnp.einsum('bgrqk,bgkd->bgrqd', p.astype(vb.dtype), vb)

    o = lax.map(block, jnp.arange(T // Q_BLOCK))
    return jnp.moveaxis(o, 0, 3).reshape(B, G, R, T, Dh)


def peer_ffn(h, w_query, sub_keys, expert_u, expert_v):
    B, T, D = h.shape
    n_tok = B * T
    C = PEER_TOKEN_CHUNK
    half = PEER_D_KEY // 2

    def chunk(xc):
        qc = (xc @ w_query).reshape(C, PEER_HEADS, 2, half)
        s = jnp.einsum('chpd,hpnd->chpn', qc, sub_keys, preferred_element_type=jnp.float32)
        top_s, top_i = lax.top_k(s, PEER_TOPK)
        cand_s = (top_s[:, :, 0, :, None] + top_s[:, :, 1, None, :]).reshape(C, PEER_HEADS, PEER_TOPK * PEER_TOPK)
        cand_i = (top_i[:, :, 0, :, None] * PEER_N_KEYS + top_i[:, :, 1, None, :]).reshape(C, PEER_HEADS, PEER_TOPK * PEER_TOPK)
        best_s, best_pos = lax.top_k(cand_s, PEER_TOPK)
        expert_idx = jnp.take_along_axis(cand_i, best_pos, axis=-1)
        g = jax.nn.softmax(best_s, axis=-1)
        u = expert_u[expert_idx]
        vv = expert_v[expert_idx]
        act = jax.nn.gelu(jnp.einsum('chkd,cd->chk', u, xc, preferred_element_type=jnp.float32))
        return jnp.einsum('chk,chkd->cd', (g * act).astype(vv.dtype), vv)

    out = lax.map(chunk, h.reshape(n_tok // C, C, D))
    return out.reshape(B, T, D)


def kernel(x, norm1_g, w_in, cmp_pe, cmp_k_w1, cmp_k_w2, cmp_v_w1, cmp_v_w2, q_norm_g, k_norm_g, rel_bias_table, sb_out_g, nsa_out_g, w_out, norm2_g, peer_w_query, peer_sub_keys, peer_expert_u, peer_expert_v):
    B, T, D = x.shape
    G, R, Dh = N_NSA_KV, NSA_GROUP, HEAD_DIM
    l = 0
    x2 = x.reshape(B * T, D)

    starts = [int(c) // Dh for c in np.cumsum((0,) + IN_SPLITS)]
    _, _, _, c_qn, c_kc, c_vc, c_ks, c_vs, c_kw, c_vw, c_gate = starts[:11]
    n_pad = (-IN_COLS) % 640
    n_cols = IN_COLS + n_pad
    w_in_p = jnp.pad(w_in[l], ((0, 0), (0, n_pad))).astype(jnp.bfloat16)
    head_flag = np.zeros((n_cols,), np.float32)
    head_gain = jnp.ones((n_cols,), jnp.float32)
    for c0, width, gain in ((c_qn, NSA_WIDTH, q_norm_g[l] * (Dh ** -0.5)), (c_ks, KV_WIDTH, k_norm_g[l]),
                            (c_kw, KV_WIDTH, k_norm_g[l])):
        head_flag[c0 * Dh:c0 * Dh + width] = 1.0
        head_gain = head_gain.at[c0 * Dh:c0 * Dh + width].set(jnp.tile(gain, width // Dh))
    proj = norm_matmul(x2, norm1_g[l], w_in_p, head_gain, head_flag).reshape(B, T, n_cols)

    o_sb = stick_breaking_attention_pallas(proj, N_SB_HEADS, 0, N_SB_HEADS, 2 * N_SB_HEADS)

    kv_heads = lambda c0: proj[:, :, c0 * Dh:c0 * Dh + KV_WIDTH].astype(jnp.float32).reshape(B, T, G, Dh)
    to_bgtd = lambda t: t.transpose(0, 2, 1, 3)
    kc = to_bgtd(rms_norm(compress_blocks(kv_heads(c_kc), cmp_pe[l], cmp_k_w1[l], cmp_k_w2[l]), k_norm_g[l]))
    vc = to_bgtd(compress_blocks(kv_heads(c_vc), cmp_pe[l], cmp_v_w1[l], cmp_v_w2[l]))
    bf16 = jnp.bfloat16
    n_cmp = kc.shape[2]
    ncp = -(-n_cmp // 128) * 128
    pad_c = lambda t: jnp.pad(t, ((0, 0), (0, 0), (0, ncp - n_cmp), (0, 0))).astype(bf16)
    near_bias, far_bias, tile_bias = nsa_bias_tables(rel_bias_table, T, ncp)
    o_cmp, sel_t = compressed_attention_select(proj, c_qn, pad_c(kc), pad_c(vc), near_bias, far_bias, n_cmp)
    sel = jnp.swapaxes(sel_t, 2, 3).astype(bf16)
    o_slc = band_attention(proj, c_qn, c_ks, c_vs, tile_bias, sel)
    o_win = band_attention(proj, c_qn, c_kw, c_vw, tile_bias)
    gate_logits = proj[:, :, c_gate * Dh:c_gate * Dh + N_GATES].astype(jnp.float32)
    gates = jax.nn.sigmoid(gate_logits).reshape(B, T, N_NSA_HEADS, 3, 1)
    per_head = lambda t: t.reshape(B, T, N_NSA_HEADS, Dh)
    o_nsa = (gates[:, :, :, 0] * per_head(o_cmp) + gates[:, :, :, 1] * per_head(o_slc)
             + gates[:, :, :, 2] * per_head(o_win)).reshape(B, T, NSA_WIDTH)

    mixed = jnp.concatenate([rms_norm(o_sb, sb_out_g[l]), rms_norm(o_nsa, nsa_out_g[l])], axis=-1)
    x1 = matmul_residual(mixed.reshape(B * T, MIX_WIDTH), w_out[l].astype(jnp.bfloat16), x2)

    h2, idx_t, gate_t = peer_route(x1, norm2_g[l], peer_w_query[l], peer_sub_keys[l])
    uv = pack_expert_tables(peer_expert_u[l], peer_expert_v[l])
    out = peer_experts(x1, h2.T, idx_t.T, gate_t.T, uv)
    return out.reshape(B, T, D)
```

```python
import functools
import math

import numpy as np
import jax
import jax.numpy as jnp
from jax import lax
from jax.experimental import pallas as pl
from jax.experimental.pallas import tpu as pltpu

D_MODEL = 2048
HEAD_DIM = 128
N_SB_HEADS = 8
N_NSA_HEADS = 8
N_NSA_KV = 2
NSA_GROUP = N_NSA_HEADS // N_NSA_KV
SB_WIDTH = N_SB_HEADS * HEAD_DIM
NSA_WIDTH = N_NSA_HEADS * HEAD_DIM
KV_WIDTH = N_NSA_KV * HEAD_DIM
MIX_WIDTH = SB_WIDTH + NSA_WIDTH
N_GATES = 3 * N_NSA_HEADS
IN_SPLITS = (SB_WIDTH, SB_WIDTH, SB_WIDTH, NSA_WIDTH, KV_WIDTH, KV_WIDTH, KV_WIDTH, KV_WIDTH, KV_WIDTH, KV_WIDTH, N_GATES)
IN_COLS = sum(IN_SPLITS)
Q_BLOCK = 128
CMP_BLOCK = 32
CMP_STRIDE = 16
CMP_HIDDEN = 256
SEL_BLOCK = 64
N_SELECT = 16
SEL_Q_CHUNK = 32
WINDOW = 512
N_BUCKETS = 32
MAX_DISTANCE = 128
PEER_HEADS = 8
PEER_N_KEYS = 128
PEER_D_KEY = 256
PEER_TOPK = 16
PEER_TOKEN_CHUNK = 128
EPS = 1e-6
NEG = -1e30
BIG = 1e30

VMEM_LIMIT_BYTES = 56 * 1024 * 1024


def _norm_matmul_kernel(x_ref, g_ref, w_ref, hg_ref, hf_ref, o_ref, h_ref, *, first_head_norm_tile):
    j = pl.program_id(1)

    @pl.when(j == 0)
    def _():
        x = x_ref[...]
        ms = jnp.mean(x * x, axis=-1, keepdims=True)
        h_ref[...] = (x * lax.rsqrt(ms + EPS) * g_ref[...]).astype(jnp.bfloat16)

    @pl.when(j < first_head_norm_tile)
    def _():
        o_ref[...] = jnp.dot(h_ref[...], w_ref[...], preferred_element_type=jnp.float32).astype(o_ref.dtype)

    @pl.when(j >= first_head_norm_tile)
    def _():
        acc = jnp.dot(h_ref[...], w_ref[...], preferred_element_type=jnp.float32)
        for hb in range(o_ref.shape[1] // HEAD_DIM):
            cols = slice(hb * HEAD_DIM, (hb + 1) * HEAD_DIM)
            y = acc[:, cols]
            ms = jnp.mean(y * y, axis=-1, keepdims=True)
            yn = y * lax.rsqrt(ms + EPS) * hg_ref[:, cols]
            o_ref[:, cols] = jnp.where(hf_ref[:, cols] > 0.5, yn, y).astype(o_ref.dtype)


def norm_matmul(x, g, w, head_gain, head_flag, tm=1024, tn=640, out_dtype=jnp.bfloat16):
    m, k = x.shape
    n = w.shape[1]
    tm = min(tm, m)
    assert m % tm == 0 and n % tn == 0 and tn % HEAD_DIM == 0
    first_flagged = int(np.argmax(np.asarray(head_flag) > 0.5)) if np.any(np.asarray(head_flag) > 0.5) else n
    return pl.pallas_call(
        functools.partial(_norm_matmul_kernel, first_head_norm_tile=first_flagged // tn),
        grid=(m // tm, n // tn),
        in_specs=[
            pl.BlockSpec((tm, k), lambda i, j: (i, 0)),
            pl.BlockSpec((1, k), lambda i, j: (0, 0)),
            pl.BlockSpec((k, tn), lambda i, j: (0, j)),
            pl.BlockSpec((1, tn), lambda i, j: (0, j)),
            pl.BlockSpec((1, tn), lambda i, j: (0, j)),
        ],
        out_specs=pl.BlockSpec((tm, tn), lambda i, j: (i, j)),
        out_shape=jax.ShapeDtypeStruct((m, n), out_dtype),
        scratch_shapes=[pltpu.VMEM((tm, k), jnp.bfloat16)],
        compiler_params=pltpu.CompilerParams(
            dimension_semantics=("parallel", "arbitrary"),
            vmem_limit_bytes=VMEM_LIMIT_BYTES),
        name="norm_matmul",
    )(x, g.reshape(1, k), w, head_gain.reshape(1, n), jnp.asarray(head_flag, jnp.float32).reshape(1, n))


N_BRANCHES = N_GATES // N_NSA_HEADS


def _mix_out_kernel(sb_ref, cmp_ref, slc_ref, win_ref, gl_ref, gsb_ref, gnsa_ref, w_ref, x_ref, o_ref, a_ref):
    @pl.when(pl.program_id(1) == 0)
    def _():
        width = sb_ref.shape[1]
        gate = jax.nn.sigmoid(gl_ref[...].astype(jnp.float32))
        g_hi = gate.astype(jnp.bfloat16)
        g_lo = (gate - g_hi.astype(jnp.float32)).astype(jnp.bfloat16)
        gate_col = lax.broadcasted_iota(jnp.int32, (gl_ref.shape[1], width), 0)
        head = lax.broadcasted_iota(jnp.int32, (gl_ref.shape[1], width), 1) // HEAD_DIM
        o_nsa = jnp.zeros((sb_ref.shape[0], width), jnp.float32)
        for b, branch_ref in enumerate((cmp_ref, slc_ref, win_ref)):
            spread = jnp.where(gate_col == head * N_BRANCHES + b, 1.0, 0.0).astype(jnp.bfloat16)
            g = (jnp.dot(g_hi, spread, preferred_element_type=jnp.float32)
                 + jnp.dot(g_lo, spread, preferred_element_type=jnp.float32))
            o_nsa = o_nsa + g * branch_ref[...]

        def norm(v, gain):
            return v * lax.rsqrt(jnp.mean(v * v, axis=-1, keepdims=True) + EPS) * gain

        a_ref[:, :width] = norm(sb_ref[...], gsb_ref[...]).astype(jnp.bfloat16)
        a_ref[:, width:] = norm(o_nsa, gnsa_ref[...]).astype(jnp.bfloat16)

    o_ref[...] = x_ref[...] + jnp.dot(a_ref[...], w_ref[...], preferred_element_type=jnp.float32)


def mix_out_projection(o_sb, o_cmp, o_slc, o_win, proj, gate_col, g_sb, g_nsa, w, x, tm=512, tn=512):
    m, width = o_sb.shape
    n = w.shape[1]
    tm = min(tm, m)
    assert m % tm == 0 and n % tn == 0 and N_GATES <= LANES
    head_out = pl.BlockSpec((tm, width), lambda i, j: (i, 0))
    gain = pl.BlockSpec((1, width), lambda i, j: (0, 0))
    return pl.pallas_call(
        _mix_out_kernel,
        grid=(m // tm, n // tn),
        in_specs=[
            head_out, head_out, head_out, head_out,
            pl.BlockSpec((tm, LANES), lambda i, j: (i, gate_col)),
            gain, gain,
            pl.BlockSpec((2 * width, tn), lambda i, j: (0, j)),
            pl.BlockSpec((tm, tn), lambda i, j: (i, j)),
        ],
        out_specs=pl.BlockSpec((tm, tn), lambda i, j: (i, j)),
        out_shape=jax.ShapeDtypeStruct((m, n), jnp.float32),
        scratch_shapes=[pltpu.VMEM((tm, 2 * width), jnp.bfloat16)],
        compiler_params=pltpu.CompilerParams(
            dimension_semantics=("parallel", "arbitrary"),
            vmem_limit_bytes=VMEM_LIMIT_BYTES),
        name="mix_out_projection",
    )(o_sb, o_cmp, o_slc, o_win, proj, g_sb.reshape(1, width), g_nsa.reshape(1, width), w, x)


SB_Q_TILE = 256
SB_K_TILE = 128
F32_EXP_UNDERFLOW = -104.0


def _sb_attn_kernel(q_ref, k_ref, v_ref, o_ref, c_ref, acc_ref):
    tq, dh = q_ref.shape
    tk = SB_K_TILE
    qi = pl.program_id(2)
    q = q_ref[...]
    scale = dh ** -0.5
    c_ref[...] = jnp.zeros(c_ref.shape, jnp.float32)
    acc_ref[...] = jnp.zeros(acc_ref.shape, jnp.float32)
    jj = lax.broadcasted_iota(jnp.int32, (2 * tk, 2 * tk), 0) % tk
    ss = lax.broadcasted_iota(jnp.int32, (2 * tk, 2 * tk), 1)
    suffix_and_total = jnp.where((jj > ss) | (ss >= tk), 1.0, 0.0).astype(jnp.bfloat16)
    row = lax.broadcasted_iota(jnp.int32, (tq, tk), 0)
    col = lax.broadcasted_iota(jnp.int32, (tq, tk), 1)

    def tile_terms(kj, masked, may_be_absent):
        live = (kj >= 0).astype(jnp.float32) if may_be_absent else 1.0
        start = pl.multiple_of(jnp.maximum(kj, 0) * tk, tk)
        k = k_ref[pl.ds(start, tk), :]
        v = v_ref[pl.ds(start, tk), :]
        z = lax.dot_general(q, k, (((1,), (1,)), ((), ())), preferred_element_type=jnp.float32) * scale
        t = jnp.log(1.0 + jnp.exp(-jnp.abs(z)))
        log_keep = -(jnp.maximum(z, 0.0) + t) * live
        log_beta = jnp.minimum(z, 0.0) - t
        keep = live
        if masked:
            before = (kj * tk + col) < (qi * tq + row)
            log_keep = jnp.where(before, log_keep, 0.0)
            keep = jnp.where(before, live, 0.0)
        hi = log_keep.astype(jnp.bfloat16)
        lo = (log_keep - hi.astype(jnp.float32)).astype(jnp.bfloat16)
        sums = jnp.dot(jnp.concatenate([hi, lo], axis=1), suffix_and_total, preferred_element_type=jnp.float32)
        return log_beta + sums[:, :tk], sums[:, tk:], keep, v

    def tile_pair(kj, masked):
        lw_a, tot_a, keep_a, v_a = tile_terms(kj, masked, False)
        lw_b, tot_b, keep_b, v_b = tile_terms(kj - 1, masked, not masked)
        c = c_ref[...]
        a_a = jnp.exp(lw_a + c)
        if masked:
            a_a = a_a * keep_a
        a_b = jnp.exp(lw_b + (c + tot_a)) * keep_b
        a = jnp.concatenate([a_a, a_b], axis=1).astype(jnp.bfloat16)
        acc_ref[...] += jnp.dot(a, jnp.concatenate([v_a, v_b], axis=0), preferred_element_type=jnp.float32)
        c_new = c + tot_a + tot_b
        c_ref[...] = c_new
        return jnp.max(c_new)

    n_diag = tq // tk
    assert n_diag == 2
    c_max = tile_pair(qi * n_diag + 1, True)

    def cond(carry):
        kj, c_max = carry
        return (kj >= 0) & (c_max > F32_EXP_UNDERFLOW)

    def body(carry):
        kj, _ = carry
        return kj - 2, tile_pair(kj, False)

    lax.while_loop(cond, body, (qi * n_diag - 1, c_max))
    o_ref[...] = acc_ref[...]


def stick_breaking_attention_pallas(proj, n_heads, q_col, k_col, v_col):
    B, T, _ = proj.shape
    dh = HEAD_DIM
    tq = min(SB_Q_TILE, T)
    return pl.pallas_call(
        _sb_attn_kernel,
        grid=(B, n_heads, T // tq),
        in_specs=[
            pl.BlockSpec((None, tq, dh), lambda b, h, i: (b, i, q_col + h)),
            pl.BlockSpec((None, T, dh), lambda b, h, i: (b, 0, k_col + h)),
            pl.BlockSpec((None, T, dh), lambda b, h, i: (b, 0, v_col + h)),
        ],
        out_specs=pl.BlockSpec((None, tq, dh), lambda b, h, i: (b, i, h)),
        out_shape=jax.ShapeDtypeStruct((B, T, n_heads * dh), jnp.float32),
        scratch_shapes=[
            pltpu.VMEM((tq, SB_K_TILE), jnp.float32),
            pltpu.VMEM((tq, dh), jnp.float32),
        ],
        compiler_params=pltpu.CompilerParams(
            dimension_semantics=("parallel", "parallel", "arbitrary"),
            vmem_limit_bytes=VMEM_LIMIT_BYTES),
        name="sb_attention",
    )(proj, proj, proj)


def _compress_kernel(hb_ref, w1_ref, w2_ref, pe_ref, g_ref, o_ref, *, n_cmp):
    is_key = (pl.program_id(0) == 0).astype(jnp.float32)
    hb = hb_ref[...]
    nh, half = hb.shape
    a = jnp.dot(hb, w1_ref[:half, :], preferred_element_type=jnp.float32)
    b = jnp.dot(hb, w1_ref[half:, :], preferred_element_type=jnp.float32)
    pe_term = jnp.dot(pe_ref[...], w1_ref[...], preferred_element_type=jnp.float32)[0:1, :]
    pre = a + pltpu.roll(b, shift=nh - 1, axis=0) + pe_term
    y = jnp.dot(jax.nn.gelu(pre).astype(jnp.bfloat16), w2_ref[...], preferred_element_type=jnp.float32)
    yn = y * lax.rsqrt(jnp.mean(y * y, axis=-1, keepdims=True) + EPS) * g_ref[...]
    y = y + is_key * (yn - y)
    row = lax.broadcasted_iota(jnp.int32, y.shape, 0)
    o_ref[...] = jnp.where(row < n_cmp, y, 0.0).astype(o_ref.dtype)


def compress_kv(kv, pe, k_w1, k_w2, v_w1, v_w2, k_gain):
    _, B, G, T, Dh = kv.shape
    assert CMP_BLOCK == 2 * CMP_STRIDE and T % CMP_STRIDE == 0
    nh = T // CMP_STRIDE
    n_cmp = (T - CMP_BLOCK) // CMP_STRIDE + 1
    half = CMP_STRIDE * Dh
    bf16 = jnp.bfloat16
    hb = kv.reshape(2, B, G, nh, half)
    w1 = jnp.stack([k_w1, v_w1]).astype(bf16)
    w2 = jnp.stack([k_w2, v_w2]).astype(bf16)
    pe8 = jnp.broadcast_to(pe.reshape(1, 2 * half), (8, 2 * half)).astype(bf16)
    hid = w2.shape[1]
    return pl.pallas_call(
        functools.partial(_compress_kernel, n_cmp=n_cmp),
        grid=(2, B, G),
        in_specs=[
            pl.BlockSpec((None, None, None, nh, half), lambda w, b, g: (w, b, g, 0, 0)),
            pl.BlockSpec((None, 2 * half, hid), lambda w, b, g: (w, 0, 0)),
            pl.BlockSpec((None, hid, Dh), lambda w, b, g: (w, 0, 0)),
            pl.BlockSpec((8, 2 * half), lambda w, b, g: (0, 0)),
            pl.BlockSpec((1, Dh), lambda w, b, g: (0, 0)),
        ],
        out_specs=pl.BlockSpec((None, None, None, nh, Dh), lambda w, b, g: (w, b, g, 0, 0)),
        out_shape=jax.ShapeDtypeStruct((2, B, G, nh, Dh), bf16),
        compiler_params=pltpu.CompilerParams(
            dimension_semantics=("parallel", "parallel", "parallel"),
            vmem_limit_bytes=VMEM_LIMIT_BYTES),
        name="compress_kv",
    )(hb, w1, w2, pe8, k_gain.reshape(1, Dh)), n_cmp


def _bias_saturation_distance():
    max_exact = N_BUCKETS // 2
    return int(math.ceil(max_exact * (MAX_DISTANCE / max_exact) ** ((N_BUCKETS - 1 - max_exact) / (N_BUCKETS - max_exact))))


def _cmp_attn_kernel(q_ref, kc_ref, vc_ref, near_ref, far_ref, ov_ref, o_ref, selt_ref, *, n_cmp, n_sel):
    qb = q_ref.shape[0]
    ncp, dh = kc_ref.shape
    r = q_ref.shape[1] // dh
    nb = ov_ref.shape[0]
    qi = pl.program_id(2)
    q = _stack_heads(q_ref[...], r)
    s = lax.dot_general(q, kc_ref[...], (((1,), (1,)), ((), ())), preferred_element_type=jnp.float32)
    wn = near_ref.shape[-1]
    first = qi * (qb // CMP_STRIDE) - qb // CMP_STRIDE
    place = jnp.where(lax.broadcasted_iota(jnp.int32, (wn, ncp), 1)
                      == first + lax.broadcasted_iota(jnp.int32, (wn, ncp), 0), 1.0, 0.0).astype(jnp.bfloat16)
    near = near_ref[...].reshape(r * qb, wn)
    near_hi = near.astype(jnp.bfloat16)
    near_lo = (near - near_hi.astype(jnp.float32)).astype(jnp.bfloat16)
    s = s + (jnp.dot(near_hi, place, preferred_element_type=jnp.float32)
             + jnp.dot(near_lo, place, preferred_element_type=jnp.float32))
    s = s.reshape(r, qb, ncp) + far_ref[...]
    tpos = qi * qb + lax.broadcasted_iota(jnp.int32, (qb, ncp), 0)
    c = lax.broadcasted_iota(jnp.int32, (qb, ncp), 1)
    valid = ((tpos - (c * CMP_STRIDE + CMP_BLOCK - 1)) >= 0) & (c < n_cmp)
    s = jnp.where(valid[None], s, NEG)
    m = jnp.max(s, axis=-1, keepdims=True)
    e = jnp.where(valid[None], jnp.exp(s - m), 0.0)
    l = jnp.sum(e, axis=-1, keepdims=True)
    p = e * jnp.where(l > 0.0, 1.0 / l, 0.0)
    o = jnp.dot(p.reshape(r * qb, ncp).astype(jnp.bfloat16), vc_ref[...], preferred_element_type=jnp.float32)
    o_ref[...] = _unstack_heads(o, r)

    psum = jnp.sum(p, axis=0)
    p_hi = psum.astype(jnp.bfloat16)
    p_lo = (psum - p_hi.astype(jnp.float32)).astype(jnp.bfloat16)
    nt = (((1,), (1,)), ((), ()))
    imp_t = (lax.dot_general(ov_ref[...], p_hi, nt, preferred_element_type=jnp.float32)
             + lax.dot_general(ov_ref[...], p_lo, nt, preferred_element_type=jnp.float32))

    blk = lax.broadcasted_iota(jnp.int32, (nb, qb), 0)
    cur = (qi * qb + lax.broadcasted_iota(jnp.int32, (nb, qb), 1)) // SEL_BLOCK
    forced = (blk == 0) | (blk == cur) | (blk == cur - 1)
    score = jnp.where(forced, BIG, jnp.where(blk > cur, NEG, imp_t))
    rank = jnp.zeros((nb, qb), jnp.float32)
    for mblk in range(nb):
        row = score[mblk:mblk + 1, :]
        ahead = (row > score) | ((row == score) & (blk > mblk))
        rank = rank + jnp.where(ahead, 1.0, 0.0)
    selt_ref[...] = jnp.where(rank < float(n_sel), 1.0, 0.0)


def compressed_attention_select(proj, q_col, kc, vc, near_bias, far_bias, n_cmp):
    B, T, _ = proj.shape
    G, R = near_bias.shape[:2]
    Dh = HEAD_DIM
    assert q_col % R == 0
    ncp = kc.shape[2]
    nb = T // SEL_BLOCK
    n_sel = min(N_SELECT, nb)
    c0 = np.arange(ncp)[None, :] * CMP_STRIDE
    s0 = np.arange(nb)[:, None] * SEL_BLOCK
    ov = np.clip(np.minimum(c0 + CMP_BLOCK, s0 + SEL_BLOCK) - np.maximum(c0, s0), 0, None).astype(np.float32) / np.float32(CMP_BLOCK)
    ov = np.where(np.arange(ncp)[None, :] < n_cmp, ov, 0.0)
    ov_t = jnp.asarray(ov, dtype=jnp.bfloat16)
    qb = Q_BLOCK
    return pl.pallas_call(
        functools.partial(_cmp_attn_kernel, n_cmp=n_cmp, n_sel=n_sel),
        grid=(B, G, T // qb),
        in_specs=[
            pl.BlockSpec((None, qb, R * Dh), lambda b, g, i: (b, i, q_col // R + g)),
            pl.BlockSpec((None, None, ncp, Dh), lambda b, g, i: (b, g, 0, 0)),
            pl.BlockSpec((None, None, ncp, Dh), lambda b, g, i: (b, g, 0, 0)),
            pl.BlockSpec((None, R, qb, LANES), lambda b, g, i: (g, 0, 0, 0)),
            pl.BlockSpec((None, R, 1, ncp), lambda b, g, i: (g, 0, 0, 0)),
            pl.BlockSpec((nb, ncp), lambda b, g, i: (0, 0)),
        ],
        out_specs=[
            pl.BlockSpec((None, qb, R * Dh), lambda b, g, i: (b, i, g)),
            pl.BlockSpec((None, None, nb, qb), lambda b, g, i: (b, g, 0, i)),
        ],
        out_shape=[
            jax.ShapeDtypeStruct((B, T, G * R * Dh), jnp.float32),
            jax.ShapeDtypeStruct((B, G, nb, T), jnp.float32),
        ],
        compiler_params=pltpu.CompilerParams(
            dimension_semantics=("parallel", "parallel", "arbitrary"),
            vmem_limit_bytes=VMEM_LIMIT_BYTES),
        name="cmp_attention_select",
    )(proj, kc, vc, near_bias, far_bias, ov_t)


BAND_SLC_TILES_PER_STEP = 4


def _band_attn_kernel(*refs, mode):
    if mode == "slc":
        q_ref, k_ref, v_ref, bias_ref, sel_ref, o_ref, m_ref, l_ref, acc_ref = refs
    else:
        q_ref, k_ref, v_ref, bias_ref, o_ref, m_ref, l_ref, acc_ref = refs
    qb = q_ref.shape[0]
    dh = k_ref.shape[1]
    r = q_ref.shape[1] // dh
    qi = pl.program_id(2)
    q = _stack_heads(q_ref[...], r)
    m_ref[...] = jnp.full(m_ref.shape, -jnp.inf, jnp.float32)
    l_ref[...] = jnp.zeros(l_ref.shape, jnp.float32)
    acc_ref[...] = jnp.zeros(acc_ref.shape, jnp.float32)
    row = lax.broadcasted_iota(jnp.int32, (qb, qb), 0)
    col = lax.broadcasted_iota(jnp.int32, (qb, qb), 1)
    if mode == "slc":
        nb = sel_ref.shape[1]
        sel = sel_ref[...]
        blk_of_col = lax.broadcasted_iota(jnp.int32, (nb, qb), 1) // SEL_BLOCK
        blk_row = lax.broadcasted_iota(jnp.int32, (nb, qb), 0)
        n_tiles = qi + 1
        kt = BAND_SLC_TILES_PER_STEP
    else:
        n_tiles = jnp.minimum(qi, WINDOW // qb) + 1
        kt = WINDOW // qb + 1
    n_steps = (n_tiles + kt - 1) // kt

    def body(step, carry):
        s_parts, v_parts = [], []
        for u in range(kt):
            off = step * kt + u
            kj = qi - off
            dead = jnp.where(off < n_tiles, 0, 1 << 20)
            start = pl.multiple_of(jnp.maximum(kj, 0) * qb, qb)
            k = k_ref[pl.ds(start, qb), :]
            s = lax.dot_general(q, k, (((1,), (1,)), ((), ())), preferred_element_type=jnp.float32)
            dist = off * qb + row - col
            if mode == "slc":
                expand = jnp.where(blk_row == blk_of_col + kj * (qb // SEL_BLOCK), 1.0, 0.0).astype(jnp.bfloat16)
                picked = jnp.dot(sel, expand, preferred_element_type=jnp.float32)
                valid = (picked > 0.5) & (dist - dead >= 0)
            else:
                valid = (dist >= 0) & (dist + dead < WINDOW)
            addmask = jnp.where(valid, 0.0, NEG)
            s = s.reshape(r, qb, qb) + (bias_ref[jnp.minimum(off, 2)] + addmask[None])
            s_parts.append(s.reshape(r * qb, qb))
            v_parts.append(v_ref[pl.ds(start, qb), :])
        s = jnp.concatenate(s_parts, axis=1)
        v = jnp.concatenate(v_parts, axis=0)
        m_prev = m_ref[...]
        m_new = jnp.maximum(m_prev, jnp.max(s, axis=-1, keepdims=True))
        alpha = jnp.exp(m_prev - m_new)
        p = jnp.exp(s - m_new)
        l_ref[...] = alpha * l_ref[...] + jnp.sum(p, axis=-1, keepdims=True)
        acc_ref[...] = alpha * acc_ref[...] + jnp.dot(p.astype(jnp.bfloat16), v, preferred_element_type=jnp.float32)
        m_ref[...] = m_new
        return carry

    lax.fori_loop(0, n_steps, body, 0)
    o_ref[...] = _unstack_heads(acc_ref[...] / l_ref[...], r)


def _stack_heads(x, r):
    dh = x.shape[1] // r
    return jnp.concatenate([x[:, h * dh:(h + 1) * dh] for h in range(r)], axis=0)


def _unstack_heads(x, r):
    qb = x.shape[0] // r
    return jnp.concatenate([x[h * qb:(h + 1) * qb, :] for h in range(r)], axis=1)


def band_attention(proj, q_col, k_col, v_col, tile_bias, sel=None):
    B, T, _ = proj.shape
    _, G, R, qb, _ = tile_bias.shape
    Dh = HEAD_DIM
    assert q_col % R == 0
    mode = "win" if sel is None else "slc"
    in_specs = [
        pl.BlockSpec((None, qb, R * Dh), lambda b, g, i: (b, i, q_col // R + g)),
        pl.BlockSpec((None, T, Dh), lambda b, g, i: (b, 0, k_col + g)),
        pl.BlockSpec((None, T, Dh), lambda b, g, i: (b, 0, v_col + g)),
        pl.BlockSpec((3, None, R, qb, qb), lambda b, g, i: (0, g, 0, 0, 0)),
    ]
    args = [proj, proj, proj, tile_bias]
    if sel is not None:
        nb = sel.shape[-1]
        in_specs.append(pl.BlockSpec((None, None, qb, nb), lambda b, g, i: (b, g, i, 0)))
        args.append(sel)
    return pl.pallas_call(
        functools.partial(_band_attn_kernel, mode=mode),
        grid=(B, G, T // qb),
        in_specs=in_specs,
        out_specs=pl.BlockSpec((None, qb, R * Dh), lambda b, g, i: (b, i, g)),
        out_shape=jax.ShapeDtypeStruct((B, T, G * R * Dh), jnp.float32),
        scratch_shapes=[
            pltpu.VMEM((R * qb, 1), jnp.float32),
            pltpu.VMEM((R * qb, 1), jnp.float32),
            pltpu.VMEM((R * qb, Dh), jnp.float32),
        ],
        compiler_params=pltpu.CompilerParams(
            dimension_semantics=("parallel", "parallel", "arbitrary"),
            vmem_limit_bytes=VMEM_LIMIT_BYTES),
        name="band_attention_" + mode,
    )(*args)


def nsa_bias_tables(table, T, ncp):
    sat = _bias_saturation_distance()
    assert sat <= Q_BLOCK + 1
    G, R = N_NSA_KV, NSA_GROUP
    back = Q_BLOCK // CMP_STRIDE
    width = 2 * back
    assert (back + 1) * CMP_STRIDE - (CMP_BLOCK - 1) >= sat and width <= LANES
    assert (Q_BLOCK - 1) - back * CMP_STRIDE - (CMP_BLOCK - 1) < 0
    tl = jnp.arange(Q_BLOCK)
    far_row = table[N_BUCKETS - 1]
    dist_n = tl[:, None] - (jnp.arange(width)[None, :] - back) * CMP_STRIDE - (CMP_BLOCK - 1)
    near = jnp.take(table, t5_bucket(dist_n), axis=0) - far_row
    near = jnp.pad(jnp.moveaxis(near, -1, 0), ((0, 0), (0, 0), (0, LANES - width))).reshape(G, R, Q_BLOCK, LANES)
    far = jnp.broadcast_to(far_row.reshape(G, R, 1, 1), (G, R, 1, ncp))
    dist_t = jnp.arange(3)[:, None, None] * Q_BLOCK + tl[None, :, None] - tl[None, None, :]
    tile_bias = jnp.moveaxis(jnp.take(table, t5_bucket(dist_t), axis=0), -1, 1).reshape(3, G, R, Q_BLOCK, Q_BLOCK)
    return near, far, tile_bias


PEER_ROUTE_TM = 256
PEER_HALF = PEER_D_KEY // 2


def _top_rows(s, order, count, payload=None):
    vals, picks = [], []
    for _ in range(count):
        m = jnp.max(s, axis=0, keepdims=True)
        o = jnp.min(jnp.where(s == m, order, jnp.inf), axis=0, keepdims=True)
        hit = order == o
        vals.append(m)
        picks.append(o if payload is None else jnp.max(jnp.where(hit, payload, -1.0), axis=0, keepdims=True))
        s = jnp.where(hit, -jnp.inf, s)
    return vals, picks


def _peer_route_kernel(x_ref, g_ref, wq_ref, keys_ref, h_ref, idx_ref, gate_ref):
    tm = x_ref.shape[0]
    nk = keys_ref.shape[1]
    topk = PEER_TOPK
    x = x_ref[...]
    ms = jnp.mean(x * x, axis=-1, keepdims=True)
    h = (x * lax.rsqrt(ms + EPS) * g_ref[...]).astype(jnp.bfloat16)
    h_ref[...] = h
    key_iota = lax.broadcasted_iota(jnp.int32, (nk, tm), 0).astype(jnp.float32)

    def head_body(head, carry):
        tops = []
        for p in range(2):
            hp = head * 2 + p
            qhp = jnp.dot(h, wq_ref[hp], preferred_element_type=jnp.float32).astype(jnp.bfloat16)
            s = lax.dot_general(keys_ref[hp], qhp, (((1,), (1,)), ((), ())), preferred_element_type=jnp.float32)
            vals, picks = _top_rows(s, key_iota, topk)
            tops.append((jnp.concatenate(vals, axis=0), jnp.concatenate(picks, axis=0)))
        (s0, i0), (s1, i1) = tops
        cs, cpos, ce = [], [], []

        def add(rows_s, rows_pos, rows_e):
            cs.append(rows_s); cpos.append(rows_pos); ce.append(rows_e)

        jr16 = lax.broadcasted_iota(jnp.int32, (topk, tm), 0).astype(jnp.float32)
        add(s0[0:1] + s1, jr16, i0[0:1] * nk + i1)
        jr8 = lax.broadcasted_iota(jnp.int32, (8, tm), 0).astype(jnp.float32)
        for i in range(1, 8):
            nj = topk // (i + 1)
            vals = s0[i:i + 1] + s1[0:8]
            add(jnp.where(jr8 < nj, vals, -jnp.inf), i * topk + jr8, i0[i:i + 1] * nk + i1[0:8])
        add(s0[8:16] + s1[0:1], (8 + jr8) * topk, i0[8:16] * nk + i1[0:1])
        cand_s = jnp.concatenate(cs, axis=0)
        cand_pos = jnp.concatenate(cpos, axis=0)
        cand_e = jnp.concatenate(ce, axis=0)
        best_s, best_e = _top_rows(cand_s, cand_pos, topk, payload=cand_e)
        bs = jnp.concatenate(best_s, axis=0)
        ex = jnp.exp(bs - bs[0:1])
        gate = ex / jnp.sum(ex, axis=0, keepdims=True)
        row0 = pl.multiple_of(head * topk, topk)
        idx_ref[pl.ds(row0, topk), :] = jnp.concatenate(best_e, axis=0).astype(jnp.int32)
        gate_ref[pl.ds(row0, topk), :] = gate
        return carry

    lax.fori_loop(0, PEER_HEADS, head_body, 0)


def peer_route(x1, g, w_query, sub_keys):
    n, d = x1.shape
    assert PEER_TOPK == 16 and PEER_N_KEYS % 8 == 0
    tm = min(PEER_ROUTE_TM, n)
    hk = PEER_HEADS * PEER_TOPK
    wq = w_query.reshape(d, 2 * PEER_HEADS, PEER_HALF).transpose(1, 0, 2).astype(jnp.bfloat16)
    keys = sub_keys.reshape(2 * PEER_HEADS, PEER_N_KEYS, PEER_HALF).astype(jnp.bfloat16)
    return pl.pallas_call(
        _peer_route_kernel,
        grid=(n // tm,),
        in_specs=[
            pl.BlockSpec((tm, d), lambda i: (i, 0)),
            pl.BlockSpec((1, d), lambda i: (0, 0)),
            pl.BlockSpec((2 * PEER_HEADS, d, PEER_HALF), lambda i: (0, 0, 0)),
            pl.BlockSpec((2 * PEER_HEADS, PEER_N_KEYS, PEER_HALF), lambda i: (0, 0, 0)),
        ],
        out_specs=[
            pl.BlockSpec((tm, d), lambda i: (i, 0)),
            pl.BlockSpec((hk, tm), lambda i: (0, i)),
            pl.BlockSpec((hk, tm), lambda i: (0, i)),
        ],
        out_shape=[
            jax.ShapeDtypeStruct((n, d), jnp.bfloat16),
            jax.ShapeDtypeStruct((hk, n), jnp.int32),
            jax.ShapeDtypeStruct((hk, n), jnp.float32),
        ],
        compiler_params=pltpu.CompilerParams(
            dimension_semantics=("parallel",),
            vmem_limit_bytes=VMEM_LIMIT_BYTES),
        name="peer_route",
    )(x1, g.reshape(1, d), wq, keys)


PEER_TB = 8
PEER_SLOTS = 3
PEER_USES = PEER_HEADS * PEER_TOPK
PEER_ROWS = PEER_TB * PEER_USES
LANES = 128


def _peer_expert_kernel(idx0_ref, idx1_ref, idx2_ref, ht_ref, gate_ref, x_ref, uv_hbm, o_ref, buf, sem, act_t):
    i = pl.program_id(0)
    n = pl.num_programs(0)
    tb, d = x_ref.shape
    ns = d // LANES
    uses = PEER_USES

    def row_copy(idx_ref, slot, tok, j):
        src = uv_hbm.at[pl.ds(pl.multiple_of(idx_ref[tok * uses + j] * ns, ns), ns)]
        return pltpu.make_async_copy(src, buf.at[slot, tok, :, j, :], sem.at[slot])

    def slot_wait(slot):
        pltpu.make_async_copy(buf.at[slot], buf.at[slot], sem.at[slot]).wait()

    @pl.when(i == 0)
    def _():
        for tok in range(tb):
            def first(j, c):
                row_copy(idx0_ref, 0, tok, j).start()
                row_copy(idx1_ref, 1, tok, j).start()
                return c
            lax.fori_loop(0, uses, first, 0)

    slot = i % PEER_SLOTS
    slot_next = (i + 2) % PEER_SLOTS
    slot_wait(slot)

    half = uses // 2
    col0 = (i % (LANES // tb)) * tb
    lane = lax.broadcasted_iota(jnp.int32, (uses, LANES), 1)

    acts = jnp.zeros((uses, LANES), jnp.float32)
    for tok in range(tb):
        for j in range(half):
            row_copy(idx2_ref, slot_next, tok, j).start()
        u = jnp.concatenate(
            [lax.bitcast_convert_type(buf[slot, tok, s] & jnp.uint32(0xFFFF0000), jnp.float32).astype(jnp.bfloat16)
             for s in range(ns)], axis=1)
        r = jnp.dot(u, ht_ref[...], preferred_element_type=jnp.float32)
        acts = jnp.where(lane == col0 + tok, r, acts)
    act_t[...] = acts.T
    act = act_t[pl.ds(pl.multiple_of(col0, tb), tb), :]
    coef = (gate_ref[...] * jax.nn.gelu(act)).astype(jnp.bfloat16)

    for tok in range(tb):
        for j in range(half, uses):
            row_copy(idx2_ref, slot_next, tok, j).start()
        for s in range(ns):
            v = lax.bitcast_convert_type(buf[slot, tok, s] << 16, jnp.float32).astype(jnp.bfloat16)
            y = jnp.dot(coef, v, preferred_element_type=jnp.float32)
            cols = slice(s * LANES, (s + 1) * LANES)
            o_ref[tok:tok + 1, cols] = x_ref[tok:tok + 1, cols] + y[tok:tok + 1, :]

    @pl.when(i == n - 1)
    def _():
        slot_wait((i + 1) % PEER_SLOTS)
        slot_wait((i + 2) % PEER_SLOTS)


def peer_experts(x1, h_t, idx, gate, uv):
    n, d = x1.shape
    tb = PEER_TB
    nblk = n // tb
    assert n % LANES == 0 and LANES % tb == 0 and d % LANES == 0 and PEER_USES == LANES
    idx_flat = idx.reshape(n * PEER_USES)
    last = nblk - 1
    smem_spec = lambda ahead: pl.BlockSpec((PEER_ROWS,), lambda i: (jnp.minimum(i + ahead, last),),
                                           memory_space=pltpu.SMEM)
    return pl.pallas_call(
        _peer_expert_kernel,
        grid=(nblk,),
        in_specs=[
            smem_spec(0), smem_spec(1), smem_spec(2),
            pl.BlockSpec((d, LANES), lambda i: (0, i // (LANES // tb))),
            pl.BlockSpec((tb, PEER_USES), lambda i: (i, 0)),
            pl.BlockSpec((tb, d), lambda i: (i, 0)),
            pl.BlockSpec(memory_space=pl.ANY),
        ],
        out_specs=pl.BlockSpec((tb, d), lambda i: (i, 0)),
        out_shape=jax.ShapeDtypeStruct((n, d), jnp.float32),
        scratch_shapes=[
            pltpu.VMEM((PEER_SLOTS, tb, d // LANES, PEER_USES, LANES), jnp.uint32),
            pltpu.SemaphoreType.DMA((PEER_SLOTS,)),
            pltpu.VMEM((LANES, PEER_USES), jnp.float32),
        ],
        compiler_params=pltpu.CompilerParams(
            dimension_semantics=("arbitrary",),
            vmem_limit_bytes=VMEM_LIMIT_BYTES),
        name="peer_experts",
    )(idx_flat, idx_flat, idx_flat, h_t, gate, x1, uv)


def pack_expert_tables(expert_u, expert_v):
    e, d = expert_u.shape
    hi = lax.bitcast_convert_type(expert_u.astype(jnp.bfloat16), jnp.uint16).astype(jnp.uint32)
    lo = lax.bitcast_convert_type(expert_v.astype(jnp.bfloat16), jnp.uint16).astype(jnp.uint32)
    return ((hi << 16) | lo).reshape(e * (d // LANES), LANES)


def rms_norm(x, g):
    xf = x.astype(jnp.float32)
    y = xf * lax.rsqrt(jnp.mean(xf * xf, axis=-1, keepdims=True) + EPS)
    return (y * g.astype(jnp.float32)).astype(x.dtype)


def masked_softmax(s, valid):
    p = jax.nn.softmax(jnp.where(valid, s, NEG), axis=-1)
    return jnp.where(valid, p, 0.0)


def t5_bucket(dist):
    max_exact = N_BUCKETS // 2
    d = jnp.maximum(dist, 0)
    log_ratio = jnp.log(jnp.maximum(d, 1).astype(jnp.float32) / max_exact) / math.log(MAX_DISTANCE / max_exact)
    large = jnp.minimum(max_exact + (log_ratio * (N_BUCKETS - max_exact)).astype(jnp.int32), N_BUCKETS - 1)
    return jnp.where(d < max_exact, d, large)


def rel_bias_grid(table, dist):
    b = jnp.take(table, t5_bucket(dist), axis=0)
    return jnp.moveaxis(b, -1, 0).reshape(N_NSA_KV, NSA_GROUP, *dist.shape)


def stick_breaking_attention(q, k, v):
    B, H, T, Dh = q.shape
    scale = Dh ** -0.5
    kpos = jnp.arange(T)

    def block(i):
        qb = lax.dynamic_slice_in_dim(q, i * Q_BLOCK, Q_BLOCK, axis=2)
        qpos = i * Q_BLOCK + jnp.arange(Q_BLOCK)
        z = jnp.einsum('bhqd,bhkd->bhqk', qb, k, preferred_element_type=jnp.float32) * scale
        before = kpos[None, :] < qpos[:, None]
        log_keep = jnp.where(before, -jax.nn.softplus(z), 0.0)
        log_between = lax.cumsum(log_keep, axis=3, reverse=True) - log_keep
        a = jnp.where(before, jnp.exp(jax.nn.log_sigmoid(z) + log_between), 0.0)
        return jnp.einsum('bhqk,bhkd->bqhd', a.astype(v.dtype), v)

    out = lax.map(block, jnp.arange(T // Q_BLOCK))
    return jnp.moveaxis(out, 0, 1).reshape(B, T, H * Dh)


def compress_blocks(kv, pe, w1, w2):
    B, T, G, Dh = kv.shape
    nc = (T - CMP_BLOCK) // CMP_STRIDE + 1
    idx = jnp.arange(nc)[:, None] * CMP_STRIDE + jnp.arange(CMP_BLOCK)[None, :]
    blocks = kv[:, idx] + pe[None, None, :, None, :]
    flat = jnp.moveaxis(blocks, 3, 2).reshape(B, nc, G, CMP_BLOCK * Dh)
    return jax.nn.gelu(flat @ w1) @ w2


def block_overlap(nc, nb):
    c0 = np.arange(nc)[:, None] * CMP_STRIDE
    s0 = np.arange(nb)[None, :] * SEL_BLOCK
    ov = np.minimum(c0 + CMP_BLOCK, s0 + SEL_BLOCK) - np.maximum(c0, s0)
    return jnp.asarray(np.clip(ov, 0, None).astype(np.float32) / np.float32(CMP_BLOCK))


def compressed_attention(q, kc, vc, table):
    B, G, R, T, Dh = q.shape
    nc = kc.shape[2]
    nb = T // SEL_BLOCK
    scale = Dh ** -0.5
    c_end = jnp.arange(nc) * CMP_STRIDE + CMP_BLOCK - 1
    overlap = block_overlap(nc, nb)

    def block(i):
        qb = lax.dynamic_slice_in_dim(q, i * Q_BLOCK, Q_BLOCK, axis=3)
        qpos = i * Q_BLOCK + jnp.arange(Q_BLOCK)
        dist = qpos[:, None] - c_end[None, :]
        s = jnp.einsum('bgrqd,bgcd->bgrqc', qb, kc, preferred_element_type=jnp.float32) * scale + rel_bias_grid(table, dist)
        p = masked_softmax(s, dist >= 0)
        o = jnp.einsum('bgrqc,bgcd->bgrqd', p.astype(vc.dtype), vc)
        imp = jnp.einsum('bgrqc,cn->bgqn', p, overlap)
        return o, imp

    o, imp = lax.map(block, jnp.arange(T // Q_BLOCK))
    return (jnp.moveaxis(o, 0, 3).reshape(B, G, R, T, Dh), jnp.moveaxis(imp, 0, 2).reshape(B, G, T, nb))


def selected_attention(q, k, v, imp, table):
    B, G, R, T, Dh = q.shape
    nb = T // SEL_BLOCK
    n_sel = min(N_SELECT, nb)
    scale = Dh ** -0.5
    blk = jnp.arange(nb)
    cur = jnp.arange(T) // SEL_BLOCK
    forced = (blk[None, :] == 0) | (blk[None, :] == cur[:, None]) | (blk[None, :] == cur[:, None] - 1)
    future = blk[None, :] > cur[:, None]
    score = jnp.where(forced, BIG, jnp.where(future, NEG, imp))
    _, sel = lax.top_k(score, n_sel)
    kb = k.reshape(B, G, nb, SEL_BLOCK, Dh)
    vb = v.reshape(B, G, nb, SEL_BLOCK, Dh)
    gather = jax.vmap(jax.vmap(lambda blocks, ix: blocks[ix]))
    tab_gr = jnp.swapaxes(table.reshape(N_BUCKETS, G, R), 0, 1)
    g_index = jnp.arange(G)[None, :, None, None]
    n_keys = n_sel * SEL_BLOCK

    def chunk(i):
        qc = lax.dynamic_slice_in_dim(q, i * SEL_Q_CHUNK, SEL_Q_CHUNK, axis=3)
        ic = lax.dynamic_slice_in_dim(sel, i * SEL_Q_CHUNK, SEL_Q_CHUNK, axis=2)
        qpos = i * SEL_Q_CHUNK + jnp.arange(SEL_Q_CHUNK)
        kg = gather(kb, ic).reshape(B, G, SEL_Q_CHUNK, n_keys, Dh)
        vg = gather(vb, ic).reshape(B, G, SEL_Q_CHUNK, n_keys, Dh)
        kpos = (ic[..., None] * SEL_BLOCK + jnp.arange(SEL_BLOCK)).reshape(B, G, SEL_Q_CHUNK, n_keys)
        dist = qpos[None, None, :, None] - kpos
        bias = jnp.moveaxis(tab_gr[g_index, t5_bucket(dist)], -1, 2)
        s = jnp.einsum('bgrqd,bgqkd->bgrqk', qc, kg, preferred_element_type=jnp.float32) * scale + bias
        p = masked_softmax(s, (dist >= 0)[:, :, None])
        return jnp.einsum('bgrqk,bgqkd->bgrqd', p.astype(vg.dtype), vg)

    o = lax.map(chunk, jnp.arange(T // SEL_Q_CHUNK))
    return jnp.moveaxis(o, 0, 3).reshape(B, G, R, T, Dh)


def window_attention(q, k, v, table):
    B, G, R, T, Dh = q.shape
    scale = Dh ** -0.5
    span = WINDOW + Q_BLOCK
    kp = jnp.pad(k, ((0, 0), (0, 0), (WINDOW, 0), (0, 0)))
    vp = jnp.pad(v, ((0, 0), (0, 0), (WINDOW, 0), (0, 0)))

    def block(i):
        qb = lax.dynamic_slice_in_dim(q, i * Q_BLOCK, Q_BLOCK, axis=3)
        kb = lax.dynamic_slice_in_dim(kp, i * Q_BLOCK, span, axis=2)
        vb = lax.dynamic_slice_in_dim(vp, i * Q_BLOCK, span, axis=2)
        qpos = i * Q_BLOCK + jnp.arange(Q_BLOCK)
        kpos = i * Q_BLOCK - WINDOW + jnp.arange(span)
        dist = qpos[:, None] - kpos[None, :]
        valid = (dist >= 0) & (dist < WINDOW) & (kpos[None, :] >= 0)
        s = jnp.einsum('bgrqd,bgkd->bgrqk', qb, kb, preferred_element_type=jnp.float32) * scale + rel_bias_grid(table, dist)
        p = masked_softmax(s, valid)
        return jnp.einsum('bgrqk,bgkd->bgrqd', p.astype(vb.dtype), vb)

    o = lax.map(block, jnp.arange(T // Q_BLOCK))
    return jnp.moveaxis(o, 0, 3).reshape(B, G, R, T, Dh)


def peer_ffn(h, w_query, sub_keys, expert_u, expert_v):
    B, T, D = h.shape
    n_tok = B * T
    C = PEER_TOKEN_CHUNK
    half = PEER_D_KEY // 2

    def chunk(xc):
        qc = (xc @ w_query).reshape(C, PEER_HEADS, 2, half)
        s = jnp.einsum('chpd,hpnd->chpn', qc, sub_keys, preferred_element_type=jnp.float32)
        top_s, top_i = lax.top_k(s, PEER_TOPK)
        cand_s = (top_s[:, :, 0, :, None] + top_s[:, :, 1, None, :]).reshape(C, PEER_HEADS, PEER_TOPK * PEER_TOPK)
        cand_i = (top_i[:, :, 0, :, None] * PEER_N_KEYS + top_i[:, :, 1, None, :]).reshape(C, PEER_HEADS, PEER_TOPK * PEER_TOPK)
        best_s, best_pos = lax.top_k(cand_s, PEER_TOPK)
        expert_idx = jnp.take_along_axis(cand_i, best_pos, axis=-1)
        g = jax.nn.softmax(best_s, axis=-1)
        u = expert_u[expert_idx]
        vv = expert_v[expert_idx]
        act = jax.nn.gelu(jnp.einsum('chkd,cd->chk', u, xc, preferred_element_type=jnp.float32))
        return jnp.einsum('chk,chkd->cd', (g * act).astype(vv.dtype), vv)

    out = lax.map(chunk, h.reshape(n_tok // C, C, D))
    return out.reshape(B, T, D)


def kernel(x, norm1_g, w_in, cmp_pe, cmp_k_w1, cmp_k_w2, cmp_v_w1, cmp_v_w2, q_norm_g, k_norm_g, rel_bias_table, sb_out_g, nsa_out_g, w_out, norm2_g, peer_w_query, peer_sub_keys, peer_expert_u, peer_expert_v):
    B, T, D = x.shape
    G, R, Dh = N_NSA_KV, NSA_GROUP, HEAD_DIM
    l = 0
    x2 = x.reshape(B * T, D)

    starts = [int(c) // Dh for c in np.cumsum((0,) + IN_SPLITS)]
    _, _, _, c_qn, c_kc, c_vc, c_ks, c_vs, c_kw, c_vw, c_gate = starts[:11]
    n_pad = (-IN_COLS) % 640
    n_cols = IN_COLS + n_pad
    w_in_p = jnp.pad(w_in[l], ((0, 0), (0, n_pad))).astype(jnp.bfloat16)
    head_flag = np.zeros((n_cols,), np.float32)
    head_gain = jnp.ones((n_cols,), jnp.float32)
    for c0, width, gain in ((c_qn, NSA_WIDTH, q_norm_g[l] * (Dh ** -0.5)), (c_ks, KV_WIDTH, k_norm_g[l]),
                            (c_kw, KV_WIDTH, k_norm_g[l])):
        head_flag[c0 * Dh:c0 * Dh + width] = 1.0
        head_gain = head_gain.at[c0 * Dh:c0 * Dh + width].set(jnp.tile(gain, width // Dh))
    proj = norm_matmul(x2, norm1_g[l], w_in_p, head_gain, head_flag).reshape(B, T, n_cols)

    o_sb = stick_breaking_attention_pallas(proj, N_SB_HEADS, 0, N_SB_HEADS, 2 * N_SB_HEADS)

    assert c_vc == c_kc + G
    kv_cmp = proj[:, :, c_kc * Dh:(c_kc + 2 * G) * Dh].reshape(B, T, 2, G, Dh).transpose(2, 0, 3, 1, 4)
    kcvc, n_cmp = compress_kv(kv_cmp, cmp_pe[l], cmp_k_w1[l], cmp_k_w2[l], cmp_v_w1[l], cmp_v_w2[l], k_norm_g[l])
    bf16 = jnp.bfloat16
    ncp = -(-n_cmp // 128) * 128
    kcvc = jnp.pad(kcvc, ((0, 0), (0, 0), (0, 0), (0, ncp - kcvc.shape[3]), (0, 0)))
    near_bias, far_bias, tile_bias = nsa_bias_tables(rel_bias_table, T, ncp)
    o_cmp, sel_t = compressed_attention_select(proj, c_qn, kcvc[0], kcvc[1], near_bias, far_bias, n_cmp)
    sel = jnp.swapaxes(sel_t, 2, 3).astype(bf16)
    o_slc = band_attention(proj, c_qn, c_ks, c_vs, tile_bias, sel)
    o_win = band_attention(proj, c_qn, c_kw, c_vw, tile_bias)
    assert SB_WIDTH == NSA_WIDTH and Dh == LANES
    flat = lambda t: t.reshape(B * T, t.shape[-1])
    x1 = mix_out_projection(flat(o_sb), flat(o_cmp), flat(o_slc), flat(o_win), flat(proj), c_gate,
                            sb_out_g[l], nsa_out_g[l], w_out[l].astype(jnp.bfloat16), x2)

    h2, idx_t, gate_t = peer_route(x1, norm2_g[l], peer_w_query[l], peer_sub_keys[l])
    uv = pack_expert_tables(peer_expert_u[l], peer_expert_v[l])
    out = peer_experts(x1, h2.T, idx_t.T, gate_t.T, uv)
    return out.reshape(B, T, D)
```

```python
import functools
import math

import numpy as np
import jax
import jax.numpy as jnp
from jax import lax
from jax.experimental import pallas as pl
from jax.experimental.pallas import tpu as pltpu

D_MODEL = 2048
HEAD_DIM = 128
N_SB_HEADS = 8
N_NSA_HEADS = 8
N_NSA_KV = 2
NSA_GROUP = N_NSA_HEADS // N_NSA_KV
SB_WIDTH = N_SB_HEADS * HEAD_DIM
NSA_WIDTH = N_NSA_HEADS * HEAD_DIM
KV_WIDTH = N_NSA_KV * HEAD_DIM
MIX_WIDTH = SB_WIDTH + NSA_WIDTH
N_GATES = 3 * N_NSA_HEADS
IN_SPLITS = (SB_WIDTH, SB_WIDTH, SB_WIDTH, NSA_WIDTH, KV_WIDTH, KV_WIDTH, KV_WIDTH, KV_WIDTH, KV_WIDTH, KV_WIDTH, N_GATES)
IN_COLS = sum(IN_SPLITS)
Q_BLOCK = 128
CMP_BLOCK = 32
CMP_STRIDE = 16
CMP_HIDDEN = 256
SEL_BLOCK = 64
N_SELECT = 16
SEL_Q_CHUNK = 32
WINDOW = 512
N_BUCKETS = 32
MAX_DISTANCE = 128
PEER_HEADS = 8
PEER_N_KEYS = 128
PEER_D_KEY = 256
PEER_TOPK = 16
PEER_TOKEN_CHUNK = 128
EPS = 1e-6
NEG = -1e30
BIG = 1e30

VMEM_LIMIT_BYTES = 56 * 1024 * 1024


def _norm_matmul_kernel(x_ref, g_ref, w_ref, hg_ref, hf_ref, o_ref, h_ref, *, first_head_norm_tile):
    j = pl.program_id(1)

    @pl.when(j == 0)
    def _():
        x = x_ref[...]
        ms = jnp.mean(x * x, axis=-1, keepdims=True)
        h_ref[...] = (x * lax.rsqrt(ms + EPS) * g_ref[...]).astype(jnp.bfloat16)

    @pl.when(j < first_head_norm_tile)
    def _():
        o_ref[...] = jnp.dot(h_ref[...], w_ref[...], preferred_element_type=jnp.float32).astype(o_ref.dtype)

    @pl.when(j >= first_head_norm_tile)
    def _():
        acc = jnp.dot(h_ref[...], w_ref[...], preferred_element_type=jnp.float32)
        for hb in range(o_ref.shape[1] // HEAD_DIM):
            cols = slice(hb * HEAD_DIM, (hb + 1) * HEAD_DIM)
            y = acc[:, cols]
            ms = jnp.mean(y * y, axis=-1, keepdims=True)
            yn = y * lax.rsqrt(ms + EPS) * hg_ref[:, cols]
            o_ref[:, cols] = jnp.where(hf_ref[:, cols] > 0.5, yn, y).astype(o_ref.dtype)


def norm_matmul(x, g, w, head_gain, head_flag, tm=1024, tn=640, out_dtype=jnp.bfloat16):
    m, k = x.shape
    n = w.shape[1]
    tm = min(tm, m)
    assert m % tm == 0 and n % tn == 0 and tn % HEAD_DIM == 0
    first_flagged = int(np.argmax(np.asarray(head_flag) > 0.5)) if np.any(np.asarray(head_flag) > 0.5) else n
    return pl.pallas_call(
        functools.partial(_norm_matmul_kernel, first_head_norm_tile=first_flagged // tn),
        grid=(m // tm, n // tn),
        in_specs=[
            pl.BlockSpec((tm, k), lambda i, j: (i, 0)),
            pl.BlockSpec((1, k), lambda i, j: (0, 0)),
            pl.BlockSpec((k, tn), lambda i, j: (0, j)),
            pl.BlockSpec((1, tn), lambda i, j: (0, j)),
            pl.BlockSpec((1, tn), lambda i, j: (0, j)),
        ],
        out_specs=pl.BlockSpec((tm, tn), lambda i, j: (i, j)),
        out_shape=jax.ShapeDtypeStruct((m, n), out_dtype),
        scratch_shapes=[pltpu.VMEM((tm, k), jnp.bfloat16)],
        compiler_params=pltpu.CompilerParams(
            dimension_semantics=("parallel", "arbitrary"),
            vmem_limit_bytes=VMEM_LIMIT_BYTES),
        name="norm_matmul",
    )(x, g.reshape(1, k), w, head_gain.reshape(1, n), jnp.asarray(head_flag, jnp.float32).reshape(1, n))


N_BRANCHES = N_GATES // N_NSA_HEADS


def _mix_out_kernel(sb_ref, cmp_ref, slc_ref, win_ref, gl_ref, gsb_ref, gnsa_ref, w_ref, x_ref, o_ref, a_ref):
    @pl.when(pl.program_id(1) == 0)
    def _():
        width = sb_ref.shape[1]
        gate = jax.nn.sigmoid(gl_ref[...].astype(jnp.float32))
        g_hi = gate.astype(jnp.bfloat16)
        g_lo = (gate - g_hi.astype(jnp.float32)).astype(jnp.bfloat16)
        gate_col = lax.broadcasted_iota(jnp.int32, (gl_ref.shape[1], width), 0)
        head = lax.broadcasted_iota(jnp.int32, (gl_ref.shape[1], width), 1) // HEAD_DIM
        o_nsa = jnp.zeros((sb_ref.shape[0], width), jnp.float32)
        for b, branch_ref in enumerate((cmp_ref, slc_ref, win_ref)):
            spread = jnp.where(gate_col == head * N_BRANCHES + b, 1.0, 0.0).astype(jnp.bfloat16)
            g = (jnp.dot(g_hi, spread, preferred_element_type=jnp.float32)
                 + jnp.dot(g_lo, spread, preferred_element_type=jnp.float32))
            o_nsa = o_nsa + g * branch_ref[...]

        def norm(v, gain):
            return v * lax.rsqrt(jnp.mean(v * v, axis=-1, keepdims=True) + EPS) * gain

        a_ref[:, :width] = norm(sb_ref[...], gsb_ref[...]).astype(jnp.bfloat16)
        a_ref[:, width:] = norm(o_nsa, gnsa_ref[...]).astype(jnp.bfloat16)

    o_ref[...] = x_ref[...] + jnp.dot(a_ref[...], w_ref[...], preferred_element_type=jnp.float32)


def mix_out_projection(o_sb, o_cmp, o_slc, o_win, proj, gate_col, g_sb, g_nsa, w, x, tm=512, tn=512):
    m, width = o_sb.shape
    n = w.shape[1]
    tm = min(tm, m)
    assert m % tm == 0 and n % tn == 0 and N_GATES <= LANES
    head_out = pl.BlockSpec((tm, width), lambda i, j: (i, 0))
    gain = pl.BlockSpec((1, width), lambda i, j: (0, 0))
    return pl.pallas_call(
        _mix_out_kernel,
        grid=(m // tm, n // tn),
        in_specs=[
            head_out, head_out, head_out, head_out,
            pl.BlockSpec((tm, LANES), lambda i, j: (i, gate_col)),
            gain, gain,
            pl.BlockSpec((2 * width, tn), lambda i, j: (0, j)),
            pl.BlockSpec((tm, tn), lambda i, j: (i, j)),
        ],
        out_specs=pl.BlockSpec((tm, tn), lambda i, j: (i, j)),
        out_shape=jax.ShapeDtypeStruct((m, n), jnp.float32),
        scratch_shapes=[pltpu.VMEM((tm, 2 * width), jnp.bfloat16)],
        compiler_params=pltpu.CompilerParams(
            dimension_semantics=("parallel", "arbitrary"),
            vmem_limit_bytes=VMEM_LIMIT_BYTES),
        name="mix_out_projection",
    )(o_sb, o_cmp, o_slc, o_win, proj, g_sb.reshape(1, width), g_nsa.reshape(1, width), w, x)


SB_Q_TILE = 256
SB_K_TILE = 128
F32_EXP_UNDERFLOW = -104.0


def _sb_attn_kernel(q_ref, k_ref, v_ref, o_ref, c_ref, acc_ref):
    tq, dh = q_ref.shape
    tk = SB_K_TILE
    qi = pl.program_id(2)
    q = q_ref[...]
    scale = dh ** -0.5
    c_ref[...] = jnp.zeros(c_ref.shape, jnp.float32)
    acc_ref[...] = jnp.zeros(acc_ref.shape, jnp.float32)
    jj = lax.broadcasted_iota(jnp.int32, (2 * tk, 2 * tk), 0) % tk
    ss = lax.broadcasted_iota(jnp.int32, (2 * tk, 2 * tk), 1)
    suffix_and_total = jnp.where((jj > ss) | (ss >= tk), 1.0, 0.0).astype(jnp.bfloat16)
    row = lax.broadcasted_iota(jnp.int32, (tq, tk), 0)
    col = lax.broadcasted_iota(jnp.int32, (tq, tk), 1)

    def tile_terms(kj, masked, may_be_absent):
        live = (kj >= 0).astype(jnp.float32) if may_be_absent else 1.0
        start = pl.multiple_of(jnp.maximum(kj, 0) * tk, tk)
        k = k_ref[pl.ds(start, tk), :]
        v = v_ref[pl.ds(start, tk), :]
        z = lax.dot_general(q, k, (((1,), (1,)), ((), ())), preferred_element_type=jnp.float32) * scale
        t = jnp.log(1.0 + jnp.exp(-jnp.abs(z)))
        log_keep = -(jnp.maximum(z, 0.0) + t) * live
        log_beta = jnp.minimum(z, 0.0) - t
        keep = live
        if masked:
            before = (kj * tk + col) < (qi * tq + row)
            log_keep = jnp.where(before, log_keep, 0.0)
            keep = jnp.where(before, live, 0.0)
        hi = log_keep.astype(jnp.bfloat16)
        lo = (log_keep - hi.astype(jnp.float32)).astype(jnp.bfloat16)
        sums = jnp.dot(jnp.concatenate([hi, lo], axis=1), suffix_and_total, preferred_element_type=jnp.float32)
        return log_beta + sums[:, :tk], sums[:, tk:], keep, v

    def tile_pair(kj, masked):
        lw_a, tot_a, keep_a, v_a = tile_terms(kj, masked, False)
        lw_b, tot_b, keep_b, v_b = tile_terms(kj - 1, masked, not masked)
        c = c_ref[...]
        a_a = jnp.exp(lw_a + c)
        if masked:
            a_a = a_a * keep_a
        a_b = jnp.exp(lw_b + (c + tot_a)) * keep_b
        a = jnp.concatenate([a_a, a_b], axis=1).astype(jnp.bfloat16)
        acc_ref[...] += jnp.dot(a, jnp.concatenate([v_a, v_b], axis=0), preferred_element_type=jnp.float32)
        c_new = c + tot_a + tot_b
        c_ref[...] = c_new
        return jnp.max(c_new)

    n_diag = tq // tk
    assert n_diag == 2
    c_max = tile_pair(qi * n_diag + 1, True)

    def cond(carry):
        kj, c_max = carry
        return (kj >= 0) & (c_max > F32_EXP_UNDERFLOW)

    def body(carry):
        kj, _ = carry
        return kj - 2, tile_pair(kj, False)

    lax.while_loop(cond, body, (qi * n_diag - 1, c_max))
    o_ref[...] = acc_ref[...]


def stick_breaking_attention_pallas(proj, n_heads, q_col, k_col, v_col):
    B, T, _ = proj.shape
    dh = HEAD_DIM
    tq = min(SB_Q_TILE, T)
    return pl.pallas_call(
        _sb_attn_kernel,
        grid=(B, n_heads, T // tq),
        in_specs=[
            pl.BlockSpec((None, tq, dh), lambda b, h, i: (b, i, q_col + h)),
            pl.BlockSpec((None, T, dh), lambda b, h, i: (b, 0, k_col + h)),
            pl.BlockSpec((None, T, dh), lambda b, h, i: (b, 0, v_col + h)),
        ],
        out_specs=pl.BlockSpec((None, tq, dh), lambda b, h, i: (b, i, h)),
        out_shape=jax.ShapeDtypeStruct((B, T, n_heads * dh), jnp.float32),
        scratch_shapes=[
            pltpu.VMEM((tq, SB_K_TILE), jnp.float32),
            pltpu.VMEM((tq, dh), jnp.float32),
        ],
        compiler_params=pltpu.CompilerParams(
            dimension_semantics=("parallel", "parallel", "arbitrary"),
            vmem_limit_bytes=VMEM_LIMIT_BYTES),
        name="sb_attention",
    )(proj, proj, proj)


def _compress_kernel(hb_ref, w1_ref, w2_ref, pe_ref, g_ref, o_ref, *, n_cmp):
    is_key = (pl.program_id(0) == 0).astype(jnp.float32)
    hb = hb_ref[...]
    nh, half = hb.shape
    a = jnp.dot(hb, w1_ref[:half, :], preferred_element_type=jnp.float32)
    b = jnp.dot(hb, w1_ref[half:, :], preferred_element_type=jnp.float32)
    pe_term = jnp.dot(pe_ref[...], w1_ref[...], preferred_element_type=jnp.float32)[0:1, :]
    pre = a + pltpu.roll(b, shift=nh - 1, axis=0) + pe_term
    y = jnp.dot(jax.nn.gelu(pre).astype(jnp.bfloat16), w2_ref[...], preferred_element_type=jnp.float32)
    yn = y * lax.rsqrt(jnp.mean(y * y, axis=-1, keepdims=True) + EPS) * g_ref[...]
    y = y + is_key * (yn - y)
    row = lax.broadcasted_iota(jnp.int32, y.shape, 0)
    o_ref[...] = jnp.where(row < n_cmp, y, 0.0).astype(o_ref.dtype)


def compress_kv(kv, pe, k_w1, k_w2, v_w1, v_w2, k_gain):
    _, B, G, T, Dh = kv.shape
    assert CMP_BLOCK == 2 * CMP_STRIDE and T % CMP_STRIDE == 0
    nh = T // CMP_STRIDE
    n_cmp = (T - CMP_BLOCK) // CMP_STRIDE + 1
    half = CMP_STRIDE * Dh
    bf16 = jnp.bfloat16
    hb = kv.reshape(2, B, G, nh, half)
    w1 = jnp.stack([k_w1, v_w1]).astype(bf16)
    w2 = jnp.stack([k_w2, v_w2]).astype(bf16)
    pe8 = jnp.broadcast_to(pe.reshape(1, 2 * half), (8, 2 * half)).astype(bf16)
    hid = w2.shape[1]
    return pl.pallas_call(
        functools.partial(_compress_kernel, n_cmp=n_cmp),
        grid=(2, B, G),
        in_specs=[
            pl.BlockSpec((None, None, None, nh, half), lambda w, b, g: (w, b, g, 0, 0)),
            pl.BlockSpec((None, 2 * half, hid), lambda w, b, g: (w, 0, 0)),
            pl.BlockSpec((None, hid, Dh), lambda w, b, g: (w, 0, 0)),
            pl.BlockSpec((8, 2 * half), lambda w, b, g: (0, 0)),
            pl.BlockSpec((1, Dh), lambda w, b, g: (0, 0)),
        ],
        out_specs=pl.BlockSpec((None, None, None, nh, Dh), lambda w, b, g: (w, b, g, 0, 0)),
        out_shape=jax.ShapeDtypeStruct((2, B, G, nh, Dh), bf16),
        compiler_params=pltpu.CompilerParams(
            dimension_semantics=("parallel", "parallel", "parallel"),
            vmem_limit_bytes=VMEM_LIMIT_BYTES),
        name="compress_kv",
    )(hb, w1, w2, pe8, k_gain.reshape(1, Dh)), n_cmp


def _bias_saturation_distance():
    max_exact = N_BUCKETS // 2
    return int(math.ceil(max_exact * (MAX_DISTANCE / max_exact) ** ((N_BUCKETS - 1 - max_exact) / (N_BUCKETS - max_exact))))


def _cmp_attn_kernel(q_ref, kc_ref, vc_ref, near_ref, far_ref, ov_ref, o_ref, selt_ref, *, n_cmp, n_sel):
    qb = q_ref.shape[0]
    ncp, dh = kc_ref.shape
    r = q_ref.shape[1] // dh
    nb = ov_ref.shape[0]
    qi = pl.program_id(2)
    q = _stack_heads(q_ref[...], r)
    s = lax.dot_general(q, kc_ref[...], (((1,), (1,)), ((), ())), preferred_element_type=jnp.float32)
    wn = near_ref.shape[-1]
    first = qi * (qb // CMP_STRIDE) - qb // CMP_STRIDE
    place = jnp.where(lax.broadcasted_iota(jnp.int32, (wn, ncp), 1)
                      == first + lax.broadcasted_iota(jnp.int32, (wn, ncp), 0), 1.0, 0.0).astype(jnp.bfloat16)
    near = near_ref[...].reshape(r * qb, wn)
    near_hi = near.astype(jnp.bfloat16)
    near_lo = (near - near_hi.astype(jnp.float32)).astype(jnp.bfloat16)
    s = s + (jnp.dot(near_hi, place, preferred_element_type=jnp.float32)
             + jnp.dot(near_lo, place, preferred_element_type=jnp.float32))
    s = s.reshape(r, qb, ncp) + far_ref[...]
    tpos = qi * qb + lax.broadcasted_iota(jnp.int32, (qb, ncp), 0)
    c = lax.broadcasted_iota(jnp.int32, (qb, ncp), 1)
    valid = ((tpos - (c * CMP_STRIDE + CMP_BLOCK - 1)) >= 0) & (c < n_cmp)
    s = jnp.where(valid[None], s, NEG)
    m = jnp.max(s, axis=-1, keepdims=True)
    e = jnp.where(valid[None], jnp.exp(s - m), 0.0)
    l = jnp.sum(e, axis=-1, keepdims=True)
    p = e * jnp.where(l > 0.0, 1.0 / l, 0.0)
    o = jnp.dot(p.reshape(r * qb, ncp).astype(jnp.bfloat16), vc_ref[...], preferred_element_type=jnp.float32)
    o_ref[...] = _unstack_heads(o, r)

    psum = jnp.sum(p, axis=0)
    p_hi = psum.astype(jnp.bfloat16)
    p_lo = (psum - p_hi.astype(jnp.float32)).astype(jnp.bfloat16)
    nt = (((1,), (1,)), ((), ()))
    imp_t = (lax.dot_general(ov_ref[...], p_hi, nt, preferred_element_type=jnp.float32)
             + lax.dot_general(ov_ref[...], p_lo, nt, preferred_element_type=jnp.float32))

    blk = lax.broadcasted_iota(jnp.int32, (nb, qb), 0)
    cur = (qi * qb + lax.broadcasted_iota(jnp.int32, (nb, qb), 1)) // SEL_BLOCK
    forced = (blk == 0) | (blk == cur) | (blk == cur - 1)
    score = jnp.where(forced, BIG, jnp.where(blk > cur, NEG, imp_t))
    rank = jnp.zeros((nb, qb), jnp.float32)
    for mblk in range(nb):
        row = score[mblk:mblk + 1, :]
        ahead = (row > score) | ((row == score) & (blk > mblk))
        rank = rank + jnp.where(ahead, 1.0, 0.0)
    selt_ref[...] = jnp.where(rank < float(n_sel), 1.0, 0.0)


def compressed_attention_select(proj, q_col, kc, vc, near_bias, far_bias, n_cmp):
    B, T, _ = proj.shape
    G, R = near_bias.shape[:2]
    Dh = HEAD_DIM
    assert q_col % R == 0
    ncp = kc.shape[2]
    nb = T // SEL_BLOCK
    n_sel = min(N_SELECT, nb)
    c0 = np.arange(ncp)[None, :] * CMP_STRIDE
    s0 = np.arange(nb)[:, None] * SEL_BLOCK
    ov = np.clip(np.minimum(c0 + CMP_BLOCK, s0 + SEL_BLOCK) - np.maximum(c0, s0), 0, None).astype(np.float32) / np.float32(CMP_BLOCK)
    ov = np.where(np.arange(ncp)[None, :] < n_cmp, ov, 0.0)
    ov_t = jnp.asarray(ov, dtype=jnp.bfloat16)
    qb = Q_BLOCK
    return pl.pallas_call(
        functools.partial(_cmp_attn_kernel, n_cmp=n_cmp, n_sel=n_sel),
        grid=(B, G, T // qb),
        in_specs=[
            pl.BlockSpec((None, qb, R * Dh), lambda b, g, i: (b, i, q_col // R + g)),
            pl.BlockSpec((None, None, ncp, Dh), lambda b, g, i: (b, g, 0, 0)),
            pl.BlockSpec((None, None, ncp, Dh), lambda b, g, i: (b, g, 0, 0)),
            pl.BlockSpec((None, R, qb, LANES), lambda b, g, i: (g, 0, 0, 0)),
            pl.BlockSpec((None, R, 1, ncp), lambda b, g, i: (g, 0, 0, 0)),
            pl.BlockSpec((nb, ncp), lambda b, g, i: (0, 0)),
        ],
        out_specs=[
            pl.BlockSpec((None, qb, R * Dh), lambda b, g, i: (b, i, g)),
            pl.BlockSpec((None, None, nb, qb), lambda b, g, i: (b, g, 0, i)),
        ],
        out_shape=[
            jax.ShapeDtypeStruct((B, T, G * R * Dh), jnp.float32),
            jax.ShapeDtypeStruct((B, G, nb, T), jnp.float32),
        ],
        compiler_params=pltpu.CompilerParams(
            dimension_semantics=("parallel", "parallel", "arbitrary"),
            vmem_limit_bytes=VMEM_LIMIT_BYTES),
        name="cmp_attention_select",
    )(proj, kc, vc, near_bias, far_bias, ov_t)


BAND_SLC_TILES_PER_STEP = 4


def _band_attn_kernel(*refs, mode):
    if mode == "slc":
        q_ref, k_ref, v_ref, bias_ref, sel_ref, o_ref, m_ref, l_ref, acc_ref = refs
    else:
        q_ref, k_ref, v_ref, bias_ref, o_ref, m_ref, l_ref, acc_ref = refs
    qb = q_ref.shape[0]
    dh = k_ref.shape[1]
    r = q_ref.shape[1] // dh
    qi = pl.program_id(2)
    q = _stack_heads(q_ref[...], r)
    m_ref[...] = jnp.full(m_ref.shape, -jnp.inf, jnp.float32)
    l_ref[...] = jnp.zeros(l_ref.shape, jnp.float32)
    acc_ref[...] = jnp.zeros(acc_ref.shape, jnp.float32)
    row = lax.broadcasted_iota(jnp.int32, (qb, qb), 0)
    col = lax.broadcasted_iota(jnp.int32, (qb, qb), 1)
    if mode == "slc":
        nb = sel_ref.shape[1]
        sel = sel_ref[...]
        blk_of_col = lax.broadcasted_iota(jnp.int32, (nb, qb), 1) // SEL_BLOCK
        blk_row = lax.broadcasted_iota(jnp.int32, (nb, qb), 0)
        n_tiles = qi + 1
        kt = BAND_SLC_TILES_PER_STEP
    else:
        n_tiles = jnp.minimum(qi, WINDOW // qb) + 1
        kt = WINDOW // qb + 1
    n_steps = (n_tiles + kt - 1) // kt

    def body(step, carry):
        s_parts, v_parts = [], []
        for u in range(kt):
            off = step * kt + u
            kj = qi - off
            dead = jnp.where(off < n_tiles, 0, 1 << 20)
            start = pl.multiple_of(jnp.maximum(kj, 0) * qb, qb)
            k = k_ref[pl.ds(start, qb), :]
            s = lax.dot_general(q, k, (((1,), (1,)), ((), ())), preferred_element_type=jnp.float32)
            dist = off * qb + row - col
            if mode == "slc":
                expand = jnp.where(blk_row == blk_of_col + kj * (qb // SEL_BLOCK), 1.0, 0.0).astype(jnp.bfloat16)
                picked = jnp.dot(sel, expand, preferred_element_type=jnp.float32)
                valid = (picked > 0.5) & (dist - dead >= 0)
            else:
                valid = (dist >= 0) & (dist + dead < WINDOW)
            addmask = jnp.where(valid, 0.0, NEG)
            s = s.reshape(r, qb, qb) + (bias_ref[jnp.minimum(off, 2)] + addmask[None])
            s_parts.append(s.reshape(r * qb, qb))
            v_parts.append(v_ref[pl.ds(start, qb), :])
        s = jnp.concatenate(s_parts, axis=1)
        v = jnp.concatenate(v_parts, axis=0)
        m_prev = m_ref[...]
        m_new = jnp.maximum(m_prev, jnp.max(s, axis=-1, keepdims=True))
        alpha = jnp.exp(m_prev - m_new)
        p = jnp.exp(s - m_new)
        l_ref[...] = alpha * l_ref[...] + jnp.sum(p, axis=-1, keepdims=True)
        acc_ref[...] = alpha * acc_ref[...] + jnp.dot(p.astype(jnp.bfloat16), v, preferred_element_type=jnp.float32)
        m_ref[...] = m_new
        return carry

    lax.fori_loop(0, n_steps, body, 0)
    o_ref[...] = _unstack_heads(acc_ref[...] / l_ref[...], r)


def _stack_heads(x, r):
    dh = x.shape[1] // r
    return jnp.concatenate([x[:, h * dh:(h + 1) * dh] for h in range(r)], axis=0)


def _unstack_heads(x, r):
    qb = x.shape[0] // r
    return jnp.concatenate([x[h * qb:(h + 1) * qb, :] for h in range(r)], axis=1)


def band_attention(proj, q_col, k_col, v_col, tile_bias, sel=None):
    B, T, _ = proj.shape
    _, G, R, qb, _ = tile_bias.shape
    Dh = HEAD_DIM
    assert q_col % R == 0
    mode = "win" if sel is None else "slc"
    in_specs = [
        pl.BlockSpec((None, qb, R * Dh), lambda b, g, i: (b, i, q_col // R + g)),
        pl.BlockSpec((None, T, Dh), lambda b, g, i: (b, 0, k_col + g)),
        pl.BlockSpec((None, T, Dh), lambda b, g, i: (b, 0, v_col + g)),
        pl.BlockSpec((3, None, R, qb, qb), lambda b, g, i: (0, g, 0, 0, 0)),
    ]
    args = [proj, proj, proj, tile_bias]
    if sel is not None:
        nb = sel.shape[-1]
        in_specs.append(pl.BlockSpec((None, None, qb, nb), lambda b, g, i: (b, g, i, 0)))
        args.append(sel)
    return pl.pallas_call(
        functools.partial(_band_attn_kernel, mode=mode),
        grid=(B, G, T // qb),
        in_specs=in_specs,
        out_specs=pl.BlockSpec((None, qb, R * Dh), lambda b, g, i: (b, i, g)),
        out_shape=jax.ShapeDtypeStruct((B, T, G * R * Dh), jnp.float32),
        scratch_shapes=[
            pltpu.VMEM((R * qb, 1), jnp.float32),
            pltpu.VMEM((R * qb, 1), jnp.float32),
            pltpu.VMEM((R * qb, Dh), jnp.float32),
        ],
        compiler_params=pltpu.CompilerParams(
            dimension_semantics=("parallel", "parallel", "arbitrary"),
            vmem_limit_bytes=VMEM_LIMIT_BYTES),
        name="band_attention_" + mode,
    )(*args)


def nsa_bias_tables(table, T, ncp):
    sat = _bias_saturation_distance()
    assert sat <= Q_BLOCK + 1
    G, R = N_NSA_KV, NSA_GROUP
    back = Q_BLOCK // CMP_STRIDE
    width = 2 * back
    assert (back + 1) * CMP_STRIDE - (CMP_BLOCK - 1) >= sat and width <= LANES
    assert (Q_BLOCK - 1) - back * CMP_STRIDE - (CMP_BLOCK - 1) < 0
    tl = jnp.arange(Q_BLOCK)
    far_row = table[N_BUCKETS - 1]
    dist_n = tl[:, None] - (jnp.arange(width)[None, :] - back) * CMP_STRIDE - (CMP_BLOCK - 1)
    near = jnp.take(table, t5_bucket(dist_n), axis=0) - far_row
    near = jnp.pad(jnp.moveaxis(near, -1, 0), ((0, 0), (0, 0), (0, LANES - width))).reshape(G, R, Q_BLOCK, LANES)
    far = jnp.broadcast_to(far_row.reshape(G, R, 1, 1), (G, R, 1, ncp))
    dist_t = jnp.arange(3)[:, None, None] * Q_BLOCK + tl[None, :, None] - tl[None, None, :]
    picked = jnp.einsum('tqkb,bh->tqkh', jax.nn.one_hot(t5_bucket(dist_t), N_BUCKETS, dtype=jnp.float32), table,
                        precision=lax.Precision.HIGHEST)
    tile_bias = jnp.moveaxis(picked, -1, 1).reshape(3, G, R, Q_BLOCK, Q_BLOCK)
    return near, far, tile_bias


PEER_ROUTE_TM = 256
PEER_HALF = PEER_D_KEY // 2


def _top_rows(s, order, count, payload=None):
    vals, picks = [], []
    for _ in range(count):
        m = jnp.max(s, axis=0, keepdims=True)
        o = jnp.min(jnp.where(s == m, order, jnp.inf), axis=0, keepdims=True)
        hit = order == o
        vals.append(m)
        picks.append(o if payload is None else jnp.max(jnp.where(hit, payload, -1.0), axis=0, keepdims=True))
        s = jnp.where(hit, -jnp.inf, s)
    return vals, picks


def _peer_route_kernel(x_ref, g_ref, wq_ref, keys_ref, ht_ref, idx_ref, gate_ref, idx_t, gate_t):
    tm = x_ref.shape[0]
    nk = keys_ref.shape[1]
    topk = PEER_TOPK
    x = x_ref[...]
    ms = jnp.mean(x * x, axis=-1, keepdims=True)
    hf = x * lax.rsqrt(ms + EPS) * g_ref[...]
    h = hf.astype(jnp.bfloat16)
    ht_ref[...] = hf.T.astype(jnp.bfloat16)
    key_iota = lax.broadcasted_iota(jnp.int32, (nk, tm), 0).astype(jnp.float32)

    def head_body(head, carry):
        tops = []
        for p in range(2):
            hp = head * 2 + p
            qhp = jnp.dot(h, wq_ref[hp], preferred_element_type=jnp.float32).astype(jnp.bfloat16)
            s = lax.dot_general(keys_ref[hp], qhp, (((1,), (1,)), ((), ())), preferred_element_type=jnp.float32)
            vals, picks = _top_rows(s, key_iota, topk)
            tops.append((jnp.concatenate(vals, axis=0), jnp.concatenate(picks, axis=0)))
        (s0, i0), (s1, i1) = tops
        cs, cpos, ce = [], [], []

        def add(rows_s, rows_pos, rows_e):
            cs.append(rows_s); cpos.append(rows_pos); ce.append(rows_e)

        jr16 = lax.broadcasted_iota(jnp.int32, (topk, tm), 0).astype(jnp.float32)
        add(s0[0:1] + s1, jr16, i0[0:1] * nk + i1)
        jr8 = lax.broadcasted_iota(jnp.int32, (8, tm), 0).astype(jnp.float32)
        for i in range(1, 8):
            nj = topk // (i + 1)
            vals = s0[i:i + 1] + s1[0:8]
            add(jnp.where(jr8 < nj, vals, -jnp.inf), i * topk + jr8, i0[i:i + 1] * nk + i1[0:8])
        add(s0[8:16] + s1[0:1], (8 + jr8) * topk, i0[8:16] * nk + i1[0:1])
        cand_s = jnp.concatenate(cs, axis=0)
        cand_pos = jnp.concatenate(cpos, axis=0)
        cand_e = jnp.concatenate(ce, axis=0)
        best_s, best_e = _top_rows(cand_s, cand_pos, topk, payload=cand_e)
        bs = jnp.concatenate(best_s, axis=0)
        ex = jnp.exp(bs - bs[0:1])
        gate = ex / jnp.sum(ex, axis=0, keepdims=True)
        row0 = pl.multiple_of(head * topk, topk)
        idx_t[pl.ds(row0, topk), :] = jnp.concatenate(best_e, axis=0)
        gate_t[pl.ds(row0, topk), :] = gate
        return carry

    lax.fori_loop(0, PEER_HEADS, head_body, 0)
    idx_ref[...] = idx_t[...].T.astype(jnp.int32)
    gate_ref[...] = gate_t[...].T


def peer_route(x1, g, w_query, sub_keys):
    n, d = x1.shape
    assert PEER_TOPK == 16 and PEER_N_KEYS % 8 == 0
    tm = min(PEER_ROUTE_TM, n)
    hk = PEER_HEADS * PEER_TOPK
    wq = w_query.reshape(d, 2 * PEER_HEADS, PEER_HALF).transpose(1, 0, 2).astype(jnp.bfloat16)
    keys = sub_keys.reshape(2 * PEER_HEADS, PEER_N_KEYS, PEER_HALF).astype(jnp.bfloat16)
    return pl.pallas_call(
        _peer_route_kernel,
        grid=(n // tm,),
        in_specs=[
            pl.BlockSpec((tm, d), lambda i: (i, 0)),
            pl.BlockSpec((1, d), lambda i: (0, 0)),
            pl.BlockSpec((2 * PEER_HEADS, d, PEER_HALF), lambda i: (0, 0, 0)),
            pl.BlockSpec((2 * PEER_HEADS, PEER_N_KEYS, PEER_HALF), lambda i: (0, 0, 0)),
        ],
        out_specs=[
            pl.BlockSpec((d, tm), lambda i: (0, i)),
            pl.BlockSpec((tm, hk), lambda i: (i, 0)),
            pl.BlockSpec((tm, hk), lambda i: (i, 0)),
        ],
        out_shape=[
            jax.ShapeDtypeStruct((d, n), jnp.bfloat16),
            jax.ShapeDtypeStruct((n, hk), jnp.int32),
            jax.ShapeDtypeStruct((n, hk), jnp.float32),
        ],
        scratch_shapes=[
            pltpu.VMEM((hk, tm), jnp.float32),
            pltpu.VMEM((hk, tm), jnp.float32),
        ],
        compiler_params=pltpu.CompilerParams(
            dimension_semantics=("parallel",),
            vmem_limit_bytes=VMEM_LIMIT_BYTES),
        name="peer_route",
    )(x1, g.reshape(1, d), wq, keys)


PEER_TB = 8
PEER_SLOTS = 3
PEER_USES = PEER_HEADS * PEER_TOPK
PEER_ROWS = PEER_TB * PEER_USES
LANES = 128


def _peer_expert_kernel(idx0_ref, idx1_ref, idx2_ref, ht_ref, gate_ref, x_ref, uv_hbm, o_ref, buf, sem, act_t):
    i = pl.program_id(0)
    n = pl.num_programs(0)
    tb, d = x_ref.shape
    ns = d // LANES
    uses = PEER_USES

    def row_copy(idx_ref, slot, tok, j):
        src = uv_hbm.at[pl.ds(pl.multiple_of(idx_ref[tok * uses + j] * ns, ns), ns)]
        return pltpu.make_async_copy(src, buf.at[slot, tok, :, j, :], sem.at[slot])

    def slot_wait(slot):
        pltpu.make_async_copy(buf.at[slot], buf.at[slot], sem.at[slot]).wait()

    @pl.when(i == 0)
    def _():
        for tok in range(tb):
            def first(j, c):
                row_copy(idx0_ref, 0, tok, j).start()
                row_copy(idx1_ref, 1, tok, j).start()
                return c
            lax.fori_loop(0, uses, first, 0)

    slot = i % PEER_SLOTS
    slot_next = (i + 2) % PEER_SLOTS
    slot_wait(slot)

    half = uses // 2
    col0 = (i % (LANES // tb)) * tb
    lane = lax.broadcasted_iota(jnp.int32, (uses, LANES), 1)

    acts = jnp.zeros((uses, LANES), jnp.float32)
    for tok in range(tb):
        for j in range(half):
            row_copy(idx2_ref, slot_next, tok, j).start()
        u = jnp.concatenate(
            [lax.bitcast_convert_type(buf[slot, tok, s] & jnp.uint32(0xFFFF0000), jnp.float32).astype(jnp.bfloat16)
             for s in range(ns)], axis=1)
        r = jnp.dot(u, ht_ref[...], preferred_element_type=jnp.float32)
        acts = jnp.where(lane == col0 + tok, r, acts)
    act_t[...] = acts.T
    act = act_t[pl.ds(pl.multiple_of(col0, tb), tb), :]
    coef = (gate_ref[...] * jax.nn.gelu(act)).astype(jnp.bfloat16)

    for tok in range(tb):
        for j in range(half, uses):
            row_copy(idx2_ref, slot_next, tok, j).start()
        for s in range(ns):
            v = lax.bitcast_convert_type(buf[slot, tok, s] << 16, jnp.float32).astype(jnp.bfloat16)
            y = jnp.dot(coef, v, preferred_element_type=jnp.float32)
            cols = slice(s * LANES, (s + 1) * LANES)
            o_ref[tok:tok + 1, cols] = x_ref[tok:tok + 1, cols] + y[tok:tok + 1, :]

    @pl.when(i == n - 1)
    def _():
        slot_wait((i + 1) % PEER_SLOTS)
        slot_wait((i + 2) % PEER_SLOTS)


def peer_experts(x1, h_t, idx, gate, uv):
    n, d = x1.shape
    tb = PEER_TB
    nblk = n // tb
    assert n % LANES == 0 and LANES % tb == 0 and d % LANES == 0 and PEER_USES == LANES
    idx_flat = idx.reshape(n * PEER_USES)
    last = nblk - 1
    smem_spec = lambda ahead: pl.BlockSpec((PEER_ROWS,), lambda i: (jnp.minimum(i + ahead, last),),
                                           memory_space=pltpu.SMEM)
    return pl.pallas_call(
        _peer_expert_kernel,
        grid=(nblk,),
        in_specs=[
            smem_spec(0), smem_spec(1), smem_spec(2),
            pl.BlockSpec((d, LANES), lambda i: (0, i // (LANES // tb))),
            pl.BlockSpec((tb, PEER_USES), lambda i: (i, 0)),
            pl.BlockSpec((tb, d), lambda i: (i, 0)),
            pl.BlockSpec(memory_space=pl.ANY),
        ],
        out_specs=pl.BlockSpec((tb, d), lambda i: (i, 0)),
        out_shape=jax.ShapeDtypeStruct((n, d), jnp.float32),
        scratch_shapes=[
            pltpu.VMEM((PEER_SLOTS, tb, d // LANES, PEER_USES, LANES), jnp.uint32),
            pltpu.SemaphoreType.DMA((PEER_SLOTS,)),
            pltpu.VMEM((LANES, PEER_USES), jnp.float32),
        ],
        compiler_params=pltpu.CompilerParams(
            dimension_semantics=("arbitrary",),
            vmem_limit_bytes=VMEM_LIMIT_BYTES),
        name="peer_experts",
    )(idx_flat, idx_flat, idx_flat, h_t, gate, x1, uv)


def pack_expert_tables(expert_u, expert_v):
    e, d = expert_u.shape
    hi = lax.bitcast_convert_type(expert_u.astype(jnp.bfloat16), jnp.uint16).astype(jnp.uint32)
    lo = lax.bitcast_convert_type(expert_v.astype(jnp.bfloat16), jnp.uint16).astype(jnp.uint32)
    return ((hi << 16) | lo).reshape(e * (d // LANES), LANES)


def rms_norm(x, g):
    xf = x.astype(jnp.float32)
    y = xf * lax.rsqrt(jnp.mean(xf * xf, axis=-1, keepdims=True) + EPS)
    return (y * g.astype(jnp.float32)).astype(x.dtype)


def masked_softmax(s, valid):
    p = jax.nn.softmax(jnp.where(valid, s, NEG), axis=-1)
    return jnp.where(valid, p, 0.0)


def t5_bucket(dist):
    max_exact = N_BUCKETS // 2
    d = jnp.maximum(dist, 0)
    log_ratio = jnp.log(jnp.maximum(d, 1).astype(jnp.float32) / max_exact) / math.log(MAX_DISTANCE / max_exact)
    large = jnp.minimum(max_exact + (log_ratio * (N_BUCKETS - max_exact)).astype(jnp.int32), N_BUCKETS - 1)
    return jnp.where(d < max_exact, d, large)


def rel_bias_grid(table, dist):
    b = jnp.take(table, t5_bucket(dist), axis=0)
    return jnp.moveaxis(b, -1, 0).reshape(N_NSA_KV, NSA_GROUP, *dist.shape)


def stick_breaking_attention(q, k, v):
    B, H, T, Dh = q.shape
    scale = Dh ** -0.5
    kpos = jnp.arange(T)

    def block(i):
        qb = lax.dynamic_slice_in_dim(q, i * Q_BLOCK, Q_BLOCK, axis=2)
        qpos = i * Q_BLOCK + jnp.arange(Q_BLOCK)
        z = jnp.einsum('bhqd,bhkd->bhqk', qb, k, preferred_element_type=jnp.float32) * scale
        before = kpos[None, :] < qpos[:, None]
        log_keep = jnp.where(before, -jax.nn.softplus(z), 0.0)
        log_between = lax.cumsum(log_keep, axis=3, reverse=True) - log_keep
        a = jnp.where(before, jnp.exp(jax.nn.log_sigmoid(z) + log_between), 0.0)
        return jnp.einsum('bhqk,bhkd->bqhd', a.astype(v.dtype), v)

    out = lax.map(block, jnp.arange(T // Q_BLOCK))
    return jnp.moveaxis(out, 0, 1).reshape(B, T, H * Dh)


def compress_blocks(kv, pe, w1, w2):
    B, T, G, Dh = kv.shape
    nc = (T - CMP_BLOCK) // CMP_STRIDE + 1
    idx = jnp.arange(nc)[:, None] * CMP_STRIDE + jnp.arange(CMP_BLOCK)[None, :]
    blocks = kv[:, idx] + pe[None, None, :, None, :]
    flat = jnp.moveaxis(blocks, 3, 2).reshape(B, nc, G, CMP_BLOCK * Dh)
    return jax.nn.gelu(flat @ w1) @ w2


def block_overlap(nc, nb):
    c0 = np.arange(nc)[:, None] * CMP_STRIDE
    s0 = np.arange(nb)[None, :] * SEL_BLOCK
    ov = np.minimum(c0 + CMP_BLOCK, s0 + SEL_BLOCK) - np.maximum(c0, s0)
    return jnp.asarray(np.clip(ov, 0, None).astype(np.float32) / np.float32(CMP_BLOCK))


def compressed_attention(q, kc, vc, table):
    B, G, R, T, Dh = q.shape
    nc = kc.shape[2]
    nb = T // SEL_BLOCK
    scale = Dh ** -0.5
    c_end = jnp.arange(nc) * CMP_STRIDE + CMP_BLOCK - 1
    overlap = block_overlap(nc, nb)

    def block(i):
        qb = lax.dynamic_slice_in_dim(q, i * Q_BLOCK, Q_BLOCK, axis=3)
        qpos = i * Q_BLOCK + jnp.arange(Q_BLOCK)
        dist = qpos[:, None] - c_end[None, :]
        s = jnp.einsum('bgrqd,bgcd->bgrqc', qb, kc, preferred_element_type=jnp.float32) * scale + rel_bias_grid(table, dist)
        p = masked_softmax(s, dist >= 0)
        o = jnp.einsum('bgrqc,bgcd->bgrqd', p.astype(vc.dtype), vc)
        imp = jnp.einsum('bgrqc,cn->bgqn', p, overlap)
        return o, imp

    o, imp = lax.map(block, jnp.arange(T // Q_BLOCK))
    return (jnp.moveaxis(o, 0, 3).reshape(B, G, R, T, Dh), jnp.moveaxis(imp, 0, 2).reshape(B, G, T, nb))


def selected_attention(q, k, v, imp, table):
    B, G, R, T, Dh = q.shape
    nb = T // SEL_BLOCK
    n_sel = min(N_SELECT, nb)
    scale = Dh ** -0.5
    blk = jnp.arange(nb)
    cur = jnp.arange(T) // SEL_BLOCK
    forced = (blk[None, :] == 0) | (blk[None, :] == cur[:, None]) | (blk[None, :] == cur[:, None] - 1)
    future = blk[None, :] > cur[:, None]
    score = jnp.where(forced, BIG, jnp.where(future, NEG, imp))
    _, sel = lax.top_k(score, n_sel)
    kb = k.reshape(B, G, nb, SEL_BLOCK, Dh)
    vb = v.reshape(B, G, nb, SEL_BLOCK, Dh)
    gather = jax.vmap(jax.vmap(lambda blocks, ix: blocks[ix]))
    tab_gr = jnp.swapaxes(table.reshape(N_BUCKETS, G, R), 0, 1)
    g_index = jnp.arange(G)[None, :, None, None]
    n_keys = n_sel * SEL_BLOCK

    def chunk(i):
        qc = lax.dynamic_slice_in_dim(q, i * SEL_Q_CHUNK, SEL_Q_CHUNK, axis=3)
        ic = lax.dynamic_slice_in_dim(sel, i * SEL_Q_CHUNK, SEL_Q_CHUNK, axis=2)
        qpos = i * SEL_Q_CHUNK + jnp.arange(SEL_Q_CHUNK)
        kg = gather(kb, ic).reshape(B, G, SEL_Q_CHUNK, n_keys, Dh)
        vg = gather(vb, ic).reshape(B, G, SEL_Q_CHUNK, n_keys, Dh)
        kpos = (ic[..., None] * SEL_BLOCK + jnp.arange(SEL_BLOCK)).reshape(B, G, SEL_Q_CHUNK, n_keys)
        dist = qpos[None, None, :, None] - kpos
        bias = jnp.moveaxis(tab_gr[g_index, t5_bucket(dist)], -1, 2)
        s = jnp.einsum('bgrqd,bgqkd->bgrqk', qc, kg, preferred_element_type=jnp.float32) * scale + bias
        p = masked_softmax(s, (dist >= 0)[:, :, None])
        return jnp.einsum('bgrqk,bgqkd->bgrqd', p.astype(vg.dtype), vg)

    o = lax.map(chunk, jnp.arange(T // SEL_Q_CHUNK))
    return jnp.moveaxis(o, 0, 3).reshape(B, G, R, T, Dh)


def window_attention(q, k, v, table):
    B, G, R, T, Dh = q.shape
    scale = Dh ** -0.5
    span = WINDOW + Q_BLOCK
    kp = jnp.pad(k, ((0, 0), (0, 0), (WINDOW, 0), (0, 0)))
    vp = jnp.pad(v, ((0, 0), (0, 0), (WINDOW, 0), (0, 0)))

    def block(i):
        qb = lax.dynamic_slice_in_dim(q, i * Q_BLOCK, Q_BLOCK, axis=3)
        kb = lax.dynamic_slice_in_dim(kp, i * Q_BLOCK, span, axis=2)
        vb = lax.dynamic_slice_in_dim(vp, i * Q_BLOCK, span, axis=2)
        qpos = i * Q_BLOCK + jnp.arange(Q_BLOCK)
        kpos = i * Q_BLOCK - WINDOW + jnp.arange(span)
        dist = qpos[:, None] - kpos[None, :]
        valid = (dist >= 0) & (dist < WINDOW) & (kpos[None, :] >= 0)
        s = jnp.einsum('bgrqd,bgkd->bgrqk', qb, kb, preferred_element_type=jnp.float32) * scale + rel_bias_grid(table, dist)
        p = masked_softmax(s, valid)
        return jnp.einsum('bgrqk,bgkd->bgrqd', p.astype(vb.dtype), vb)

    o = lax.map(block, jnp.arange(T // Q_BLOCK))
    return jnp.moveaxis(o, 0, 3).reshape(B, G, R, T, Dh)


def peer_ffn(h, w_query, sub_keys, expert_u, expert_v):
    B, T, D = h.shape
    n_tok = B * T
    C = PEER_TOKEN_CHUNK
    half = PEER_D_KEY // 2

    def chunk(xc):
        qc = (xc @ w_query).reshape(C, PEER_HEADS, 2, half)
        s = jnp.einsum('chpd,hpnd->chpn', qc, sub_keys, preferred_element_type=jnp.float32)
        top_s, top_i = lax.top_k(s, PEER_TOPK)
        cand_s = (top_s[:, :, 0, :, None] + top_s[:, :, 1, None, :]).reshape(C, PEER_HEADS, PEER_TOPK * PEER_TOPK)
        cand_i = (top_i[:, :, 0, :, None] * PEER_N_KEYS + top_i[:, :, 1, None, :]).reshape(C, PEER_HEADS, PEER_TOPK * PEER_TOPK)
        best_s, best_pos = lax.top_k(cand_s, PEER_TOPK)
        expert_idx = jnp.take_along_axis(cand_i, best_pos, axis=-1)
        g = jax.nn.softmax(best_s, axis=-1)
        u = expert_u[expert_idx]
        vv = expert_v[expert_idx]
        act = jax.nn.gelu(jnp.einsum('chkd,cd->chk', u, xc, preferred_element_type=jnp.float32))
        return jnp.einsum('chk,chkd->cd', (g * act).astype(vv.dtype), vv)

    out = lax.map(chunk, h.reshape(n_tok // C, C, D))
    return out.reshape(B, T, D)


def kernel(x, norm1_g, w_in, cmp_pe, cmp_k_w1, cmp_k_w2, cmp_v_w1, cmp_v_w2, q_norm_g, k_norm_g, rel_bias_table, sb_out_g, nsa_out_g, w_out, norm2_g, peer_w_query, peer_sub_keys, peer_expert_u, peer_expert_v):
    B, T, D = x.shape
    G, R, Dh = N_NSA_KV, NSA_GROUP, HEAD_DIM
    l = 0
    x2 = x.reshape(B * T, D)

    starts = [int(c) // Dh for c in np.cumsum((0,) + IN_SPLITS)]
    _, _, _, c_qn, c_kc, c_vc, c_ks, c_vs, c_kw, c_vw, c_gate = starts[:11]
    n_pad = (-IN_COLS) % 640
    n_cols = IN_COLS + n_pad
    w_in_p = jnp.pad(w_in[l], ((0, 0), (0, n_pad))).astype(jnp.bfloat16)
    head_flag = np.zeros((n_cols,), np.float32)
    head_gain = jnp.ones((n_cols,), jnp.float32)
    for c0, width, gain in ((c_qn, NSA_WIDTH, q_norm_g[l] * (Dh ** -0.5)), (c_ks, KV_WIDTH, k_norm_g[l]),
                            (c_kw, KV_WIDTH, k_norm_g[l])):
        head_flag[c0 * Dh:c0 * Dh + width] = 1.0
        head_gain = head_gain.at[c0 * Dh:c0 * Dh + width].set(jnp.tile(gain, width // Dh))
    proj = norm_matmul(x2, norm1_g[l], w_in_p, head_gain, head_flag).reshape(B, T, n_cols)

    o_sb = stick_breaking_attention_pallas(proj, N_SB_HEADS, 0, N_SB_HEADS, 2 * N_SB_HEADS)

    assert c_vc == c_kc + G
    kv_cmp = proj[:, :, c_kc * Dh:(c_kc + 2 * G) * Dh].reshape(B, T, 2, G, Dh).transpose(2, 0, 3, 1, 4)
    kcvc, n_cmp = compress_kv(kv_cmp, cmp_pe[l], cmp_k_w1[l], cmp_k_w2[l], cmp_v_w1[l], cmp_v_w2[l], k_norm_g[l])
    bf16 = jnp.bfloat16
    ncp = -(-n_cmp // 128) * 128
    kcvc = jnp.pad(kcvc, ((0, 0), (0, 0), (0, 0), (0, ncp - kcvc.shape[3]), (0, 0)))
    near_bias, far_bias, tile_bias = nsa_bias_tables(rel_bias_table, T, ncp)
    o_cmp, sel_t = compressed_attention_select(proj, c_qn, kcvc[0], kcvc[1], near_bias, far_bias, n_cmp)
    sel = jnp.swapaxes(sel_t, 2, 3).astype(bf16)
    o_slc = band_attention(proj, c_qn, c_ks, c_vs, tile_bias, sel)
    o_win = band_attention(proj, c_qn, c_kw, c_vw, tile_bias)
    assert SB_WIDTH == NSA_WIDTH and Dh == LANES
    flat = lambda t: t.reshape(B * T, t.shape[-1])
    x1 = mix_out_projection(flat(o_sb), flat(o_cmp), flat(o_slc), flat(o_win), flat(proj), c_gate,
                            sb_out_g[l], nsa_out_g[l], w_out[l].astype(jnp.bfloat16), x2)

    h2_t, idx, gate = peer_route(x1, norm2_g[l], peer_w_query[l], peer_sub_keys[l])
    uv = pack_expert_tables(peer_expert_u[l], peer_expert_v[l])
    out = peer_experts(x1, h2_t, idx, gate, uv)
    return out.reshape(B, T, D)
```
